```python
import jax, jax.numpy as jnp
from jax import lax
import numpy as np

D_MODEL = 1024
BATCH = 8
SEQ = 2048
DEPTH = 1

CHUNK = 64
D_PLE = 256
D_A = 512
H_A = 4
DK_A = D_A // H_A
DV_A = D_A // H_A
D_B = 512
H_B = 4
DK_B = (D_B // 2) // H_B
DV_B = D_B // H_B
GLA_RANK = 16
GLA_TAU = 16
NORM_EPS = 1e-6
IN_WIDTHS = (D_A, D_A, D_A, D_A,
             H_B * DK_B, H_B * DK_B, D_B, D_B,
             GLA_RANK,
             D_MODEL, D_MODEL)
D_IN = 4 * D_A + 2 * H_B * DK_B + 2 * D_B + GLA_RANK + 2 * D_MODEL

kernel_name = "hgrn2_gla_gated_hybrid_block"


def rms_norm(x, gain):
    xf = x.astype(jnp.float32)
    y = xf * lax.rsqrt(jnp.mean(xf * xf, axis=-1, keepdims=True) + NORM_EPS)
    return (y * gain.astype(jnp.float32)).astype(x.dtype)


def head_rms_norm(o, gain, n_heads):
    b, l, d = o.shape
    oh = o.reshape(b, l, n_heads, d // n_heads)
    return rms_norm(oh, gain.reshape(n_heads, d // n_heads)).reshape(b, l, d)


def gated_linear_attention_chunked(q, k, v, log_g):
    bsz, seq, heads, dk = q.shape
    dv = v.shape[-1]
    n_chunks = seq // CHUNK

    def to_chunks(t):
        return t.astype(jnp.float32).reshape(bsz, n_chunks, CHUNK, heads, t.shape[-1]).transpose(1, 0, 3, 2, 4)

    qc, kc, vc, gc = to_chunks(q), to_chunks(k), to_chunks(v), to_chunks(log_g)
    causal = jnp.tril(jnp.ones((CHUNK, CHUNK), dtype=bool))[:, :, None]

    def step(state, inp):
        q_, k_, v_, g_ = inp
        b = jnp.cumsum(g_, axis=2)
        o_inter = jnp.einsum('bhcd,bhde->bhce', q_ * jnp.exp(b), state)
        diff = b[:, :, :, None, :] - b[:, :, None, :, :]
        decay = jnp.exp(jnp.where(causal, diff, -jnp.inf))
        scores = jnp.sum(q_[:, :, :, None, :] * k_[:, :, None, :, :] * decay, axis=-1)
        o_intra = jnp.einsum('bhij,bhje->bhie', scores, v_)
        b_last = b[:, :, -1:, :]
        new_state = (jnp.exp(b_last[:, :, 0, :])[..., None] * state
                     + jnp.einsum('bhcd,bhce->bhde', k_ * jnp.exp(b_last - b), v_))
        return new_state, o_inter + o_intra

    s0 = jnp.zeros((bsz, heads, dk, dv), jnp.float32)
    _, o = lax.scan(step, s0, (qc, kc, vc, gc))
    return o.transpose(1, 0, 3, 2, 4).reshape(bsz, seq, heads, dv).astype(v.dtype)


def setup_inputs(seed: int = 0) -> dict:
    key = jax.random.key(seed)
    ks = jax.random.split(key, 18)
    f32 = jnp.float32
    nrm = lambda k, shape, s: jax.random.normal(k, shape, f32) * s
    return {
        "x": nrm(ks[0], (BATCH, SEQ, D_MODEL), 1.0),
        "p": nrm(ks[1], (DEPTH, BATCH, SEQ, D_PLE), 1.0),
        "w_in": nrm(ks[2], (DEPTH, D_MODEL, D_IN), D_MODEL ** -0.5),
        "lb_logits": nrm(ks[3], (DEPTH + 1, D_A), 0.5),
        "w_gla_up": nrm(ks[4], (DEPTH, GLA_RANK, H_B * DK_B), GLA_RANK ** -0.5),
        "b_gla": nrm(ks[5], (DEPTH, H_B * DK_B), 0.1),
        "norm_pre": 1.0 + nrm(ks[6], (DEPTH, D_MODEL), 0.02),
        "norm_post": 1.0 + nrm(ks[7], (DEPTH, D_MODEL), 0.02),
        "head_norm_a": 1.0 + nrm(ks[8], (DEPTH, D_A), 0.02),
        "head_norm_b": 1.0 + nrm(ks[9], (DEPTH, D_B), 0.02),
        "w_branch_a": nrm(ks[10], (DEPTH, D_A, D_MODEL), D_A ** -0.5),
        "w_branch_b": nrm(ks[11], (DEPTH, D_B, D_MODEL), D_B ** -0.5),
        "w_out": nrm(ks[12], (DEPTH, D_MODEL, D_MODEL), D_MODEL ** -0.5),
        "w_ple": nrm(ks[13], (DEPTH, D_PLE, D_MODEL), D_PLE ** -0.5),
        "w_ple_gate": nrm(ks[14], (DEPTH, D_MODEL, D_MODEL), D_MODEL ** -0.5),
        "norm_ple": 1.0 + nrm(ks[15], (DEPTH, D_MODEL), 0.02),
    }


def reference(x, p, w_in, lb_logits, w_gla_up, b_gla, norm_pre, norm_post,
              head_norm_a, head_norm_b, w_branch_a, w_branch_b, w_out,
              w_ple, w_ple_gate, norm_ple):
    bsz, seq, _ = x.shape
    split_points = [int(s) for s in np.cumsum(IN_WIDTHS)[:-1]]
    lower_bounds = jnp.cumsum(jax.nn.softmax(lb_logits.astype(jnp.float32), axis=0), axis=0)
    for i in range(DEPTH):
        h = rms_norm(x, norm_pre[i])
        proj = h @ w_in[i]
        qA, fA, iA, gA, qB, kB, vB, gB, aB, zA, zB = jnp.split(proj, split_points, axis=-1)

        lb = lower_bounds[i]
        forget = lb + (1.0 - lb) * jax.nn.sigmoid(fA.astype(jnp.float32))
        log_f = jnp.log(forget)
        k_a = 1.0 - forget
        q_a = jax.nn.silu(qA) * (DK_A ** -0.5)
        o_a = gated_linear_attention_chunked(
            q_a.reshape(bsz, seq, H_A, DK_A), k_a.reshape(bsz, seq, H_A, DK_A),
            iA.reshape(bsz, seq, H_A, DV_A), log_f.reshape(bsz, seq, H_A, DK_A)
        ).reshape(bsz, seq, D_A)
        o_a = head_rms_norm(o_a, head_norm_a[i], H_A) * jax.nn.silu(gA)

        log_alpha = jax.nn.log_sigmoid((aB @ w_gla_up[i] + b_gla[i]).astype(jnp.float32)) / GLA_TAU
        o_b = gated_linear_attention_chunked(
            (qB * (DK_B ** -0.5)).reshape(bsz, seq, H_B, DK_B), kB.reshape(bsz, seq, H_B, DK_B),
            vB.reshape(bsz, seq, H_B, DV_B), log_alpha.reshape(bsz, seq, H_B, DK_B)
        ).reshape(bsz, seq, D_B)
        o_b = head_rms_norm(o_b, head_norm_b[i], H_B) * jax.nn.silu(gB)

        y = jax.nn.sigmoid(zA) * (o_a @ w_branch_a[i]) + jax.nn.sigmoid(zB) * (o_b @ w_branch_b[i])
        y = y @ w_out[i]
        x = x + rms_norm(y, norm_post[i])

        e = rms_norm(p[i] @ w_ple[i], norm_ple[i])
        x = x + jax.nn.sigmoid(x @ w_ple_gate[i]) * e
    return x
```

```python
import functools

import numpy as np
import jax
import jax.numpy as jnp
from jax import lax
from jax.experimental import pallas as pl
from jax.experimental.pallas import tpu as pltpu

D_MODEL = 1024
D_PLE = 256
N_HEADS = 4
D_HEAD = 128
D_BR = N_HEADS * D_HEAD
DK_A = 128
DK_B = 64
GLA_RANK = 16
GLA_TAU = 16.0
NORM_EPS = 1e-6

CH = 128
N_LEVELS = 7
DIAG_LEVEL = N_LEVELS
TL = 256
RANK_PAD = 128

OFF_ZA = 0
OFF_ZB = OFF_ZA + D_MODEL
OFF_QA = OFF_ZB + D_MODEL
OFF_FA = OFF_QA + D_BR
OFF_IA = OFF_FA + D_BR
OFF_GA = OFF_IA + D_BR
OFF_QB = OFF_GA + D_BR
OFF_KB = OFF_QB + D_BR
OFF_VB = OFF_KB + D_BR
OFF_GB = OFF_VB + D_BR
OFF_AB = OFF_GB + D_BR
D_IN_PAD = OFF_AB + RANK_PAD

V7X_VMEM_LIMIT_BYTES = 56 * 1024 * 1024

_F32 = jnp.float32
_BF16 = jnp.bfloat16


def _dot(a, b):
    return jnp.dot(a, b, preferred_element_type=_F32)


def _dot_nt(a, b):
    return lax.dot_general(a, b, (((1,), (1,)), ((), ())), preferred_element_type=_F32)


def _dot_tn(a, b):
    return lax.dot_general(a, b, (((0,), (0,)), ((), ())), preferred_element_type=_F32)


def _rms(x, gain):
    ms = jnp.mean(x * x, axis=-1, keepdims=True)
    return x * lax.rsqrt(ms + NORM_EPS) * gain


def _sigmoid(x):
    return 1.0 / (1.0 + jnp.exp(-x))


def _level_constants():
    t = np.arange(CH)
    mats = []
    for l in range(N_LEVELS):
        s = 1 << l
        ref = (t & ~(2 * s - 1)) + s - 1
        odd = (t & s) != 0
        u = t[None, :]
        a = np.where(odd[:, None], (u > ref[:, None]) & (u <= t[:, None]),
                     (u > t[:, None]) & (u <= ref[:, None]))
        mats.append(a)
    mats.append(t[None, :] <= t[:, None])
    a_all = np.concatenate(mats, axis=0).astype(np.float32)
    a_cat = np.concatenate([a_all, a_all], axis=1)
    x = t[:, None] ^ t[None, :]
    lvl = np.where(x > 0, np.floor(np.log2(np.maximum(x, 1))).astype(np.int32), DIAG_LEVEL)
    lvl = np.where(t[:, None] >= t[None, :], lvl, -1).astype(np.int32)
    return a_cat, lvl


def _gla_chunk(q, k, v, g, gate, gain, amat_ref, lvl, e_ref, w_ref, st_ref, o_ref, row0):
    g_hi = g.astype(_BF16)
    g_lo = (g - g_hi.astype(_F32)).astype(_BF16)
    e_ref[...] = _dot(amat_ref[...], jnp.concatenate([g_hi, g_lo], axis=0))

    row = lax.broadcasted_iota(jnp.int32, (CH, D_BR), 0)
    for l in range(N_LEVELS):
        e = jnp.exp(e_ref[l * CH:(l + 1) * CH, :])
        odd = (row & (1 << l)) != 0
        w_ref[l] = (jnp.where(odd, q, k) * e).astype(_BF16)
    big_g = e_ref[N_LEVELS * CH:(N_LEVELS + 1) * CH, :]
    g_last = big_g[CH - 1:CH, :]
    w_ref[N_LEVELS] = q.astype(_BF16)
    w_ref[N_LEVELS + 1] = k.astype(_BF16)
    w_ref[N_LEVELS + 2] = (q * jnp.exp(big_g)).astype(_BF16)
    w_ref[N_LEVELS + 3] = (k * jnp.exp(g_last - big_g)).astype(_BF16)
    w_ref[N_LEVELS + 4] = v.astype(_BF16)
    d_last = jnp.exp(g_last)

    for h in range(N_HEADS):
        hs = slice(h * D_HEAD, (h + 1) * D_HEAD)
        sc = jnp.zeros((CH, CH), _F32)
        for l in range(N_LEVELS):
            w = w_ref[l, :, hs]
            sc = jnp.where(lvl == l, _dot_nt(w, w), sc)
        sc = jnp.where(lvl == DIAG_LEVEL, _dot_nt(w_ref[N_LEVELS, :, hs], w_ref[N_LEVELS + 1, :, hs]), sc)
        vb = w_ref[N_LEVELS + 4, :, hs]
        st = st_ref[h]
        o = _dot(sc.astype(_BF16), vb) + _dot_nt(w_ref[N_LEVELS + 2, :, hs], st.astype(_BF16))
        st_ref[h] = st * d_last[:, hs] + _dot_tn(vb, w_ref[N_LEVELS + 3, :, hs])
        gt = gate[:, hs]
        o_ref[pl.ds(row0, CH), hs] = _rms(o, gain[:, hs]) * (gt * _sigmoid(gt))


def _block_kernel(x_ref, p_ref, w_in_ref, lb_ref, w_up_ref, b_gla_ref, n_pre_ref, n_post_ref,
                  hn_a_ref, hn_b_ref, w_ba_ref, w_bb_ref, w_out_ref, w_ple_ref, w_pg_ref,
                  n_ple_ref, amat_ref, lvl_ref, out_ref,
                  proj_ref, e_ref, w_ref, st_a_ref, st_b_ref, oa_ref, ob_ref):
    @pl.when(pl.program_id(1) == 0)
    def _():
        st_a_ref[...] = jnp.zeros_like(st_a_ref)
        st_b_ref[...] = jnp.zeros_like(st_b_ref)

    x = x_ref[0]
    h = _rms(x, n_pre_ref[...]).astype(_BF16)
    proj_ref[...] = _dot(h, w_in_ref[...])

    logits = lb_ref[...]
    ex = jnp.exp(logits - jnp.max(logits, axis=0, keepdims=True))
    lb = ex[0:1, :] / jnp.sum(ex, axis=0, keepdims=True)
    lvl = lvl_ref[...]

    def chunk_body(c, carry):
        row0 = pl.multiple_of(c * CH, CH)
        rows = pl.ds(row0, CH)

        def col(off, width=D_BR):
            return proj_ref[rows, off:off + width]

        forget = lb + (1.0 - lb) * _sigmoid(col(OFF_FA))
        qa = col(OFF_QA)
        qa = qa * _sigmoid(qa) * (DK_A ** -0.5)
        _gla_chunk(qa, 1.0 - forget, col(OFF_IA), jnp.log(forget), col(OFF_GA), hn_a_ref[...],
                   amat_ref, lvl, e_ref, w_ref, st_a_ref, oa_ref, row0)

        z = _dot(col(OFF_AB, RANK_PAD).astype(_BF16), w_up_ref[...]) + b_gla_ref[...]
        log_alpha = -(jnp.maximum(-z, 0.0) + jnp.log(1.0 + jnp.exp(-jnp.abs(z)))) / GLA_TAU
        _gla_chunk(col(OFF_QB) * (DK_B ** -0.5), col(OFF_KB), col(OFF_VB), log_alpha, col(OFF_GB),
                   hn_b_ref[...], amat_ref, lvl, e_ref, w_ref, st_b_ref, ob_ref, row0)
        return carry

    lax.fori_loop(0, TL // CH, chunk_body, 0)

    ya = _dot(oa_ref[...].astype(_BF16), w_ba_ref[...])
    yb = _dot(ob_ref[...].astype(_BF16), w_bb_ref[...])
    y = _sigmoid(proj_ref[:, OFF_ZA:OFF_ZA + D_MODEL]) * ya + _sigmoid(proj_ref[:, OFF_ZB:OFF_ZB + D_MODEL]) * yb
    y = _dot(y.astype(_BF16), w_out_ref[...])
    x1 = x + _rms(y, n_post_ref[...])
    e = _rms(_dot(p_ref[0].astype(_BF16), w_ple_ref[...]), n_ple_ref[...])
    out_ref[0] = x1 + _sigmoid(_dot(x1.astype(_BF16), w_pg_ref[...])) * e


def _pad_heads(w, dk):
    lead = w.shape[:-1]
    w = w.reshape(lead + (N_HEADS, dk))
    w = jnp.pad(w, [(0, 0)] * len(lead) + [(0, 0), (0, D_HEAD - dk)])
    return w.reshape(lead + (N_HEADS * D_HEAD,))


def _const_spec(shape):
    nd = len(shape)
    return pl.BlockSpec(shape, lambda b, t: (0,) * nd, pipeline_mode=pl.Buffered(1))


def kernel(x, p, w_in, lb_logits, w_gla_up, b_gla, norm_pre, norm_post, head_norm_a, head_norm_b,
           w_branch_a, w_branch_b, w_out, w_ple, w_ple_gate, norm_ple):
    bsz, seq, _ = x.shape
    assert x.shape == (bsz, seq, D_MODEL) and seq % TL == 0 and w_in.shape[0] == 1

    wi = w_in[0]
    widths = (D_BR, D_BR, D_BR, D_BR, N_HEADS * DK_B, N_HEADS * DK_B, D_BR, D_BR, GLA_RANK, D_MODEL, D_MODEL)
    splits = [int(s) for s in np.cumsum(widths)[:-1]]
    qa, fa, ia, ga, qb, kb, vb, gb, ab, za, zb = jnp.split(wi, splits, axis=-1)
    ab = jnp.pad(ab, ((0, 0), (0, RANK_PAD - GLA_RANK)))
    w_cat = jnp.concatenate([za, zb, qa, fa, ia, ga, _pad_heads(qb, DK_B), _pad_heads(kb, DK_B), vb, gb, ab],
                            axis=-1).astype(_BF16)
    assert w_cat.shape == (D_MODEL, D_IN_PAD)
    w_up = jnp.pad(_pad_heads(w_gla_up[0], DK_B), ((0, RANK_PAD - GLA_RANK), (0, 0))).astype(_BF16)
    b_up = _pad_heads(b_gla[0], DK_B)[None, :]

    a_cat, lvl = _level_constants()
    amat = jnp.asarray(a_cat, dtype=_BF16)
    lvl = jnp.asarray(lvl)

    row = lambda a: a[0][None, :]
    operands = (
        x, p[0], w_cat, lb_logits, w_up, b_up, row(norm_pre), row(norm_post), row(head_norm_a), row(head_norm_b),
        w_branch_a[0].astype(_BF16), w_branch_b[0].astype(_BF16), w_out[0].astype(_BF16),
        w_ple[0].astype(_BF16), w_ple_gate[0].astype(_BF16), row(norm_ple), amat, lvl,
    )
    in_specs = [
        pl.BlockSpec((1, TL, D_MODEL), lambda b, t: (b, t, 0)),
        pl.BlockSpec((1, TL, D_PLE), lambda b, t: (b, t, 0)),
    ] + [_const_spec(a.shape) for a in operands[2:]]

    return pl.pallas_call(
        _block_kernel,
        grid=(bsz, seq // TL),
        in_specs=in_specs,
        out_specs=pl.BlockSpec((1, TL, D_MODEL), lambda b, t: (b, t, 0)),
        out_shape=jax.ShapeDtypeStruct(x.shape, x.dtype),
        scratch_shapes=[
            pltpu.VMEM((TL, D_IN_PAD), _F32),
            pltpu.VMEM(((N_LEVELS + 1) * CH, D_BR), _F32),
            pltpu.VMEM((N_LEVELS + 5, CH, D_BR), _BF16),
            pltpu.VMEM((N_HEADS, D_HEAD, D_HEAD), _F32),
            pltpu.VMEM((N_HEADS, D_HEAD, D_HEAD), _F32),
            pltpu.VMEM((TL, D_BR), _F32),
            pltpu.VMEM((TL, D_BR), _F32),
        ],
        compiler_params=pltpu.CompilerParams(
            dimension_semantics=("arbitrary", "arbitrary"),
            vmem_limit_bytes=V7X_VMEM_LIMIT_BYTES,
        ),
        name="hgrn2_gla_block",
    )(*operands)
```

```python
import numpy as np
import jax
import jax.numpy as jnp
from jax import lax
from jax.experimental import pallas as pl
from jax.experimental.pallas import tpu as pltpu

D_MODEL = 1024
D_PLE = 256
N_HEADS = 4
D_HEAD = 128
D_BR = N_HEADS * D_HEAD
D_PAIR = 2 * D_HEAD
DK_A = 128
DK_B = 64
GLA_RANK = 16
GLA_TAU = 16.0
NORM_EPS = 1e-6
LOG2_E = 1.4426950408889634

SUBLANES = 8
BF16_ROWS = 16

CH = 128
N_LEVELS = 7
DIAG_LEVEL = N_LEVELS
TL = 256
RANK_PAD = 128

ROW_Q = N_LEVELS
ROW_K = N_LEVELS + 1
ROW_QT = N_LEVELS + 2
ROW_KH = N_LEVELS + 3
ROW_V = N_LEVELS + 4
N_OPERAND_ROWS = N_LEVELS + 5

OFF_ZA = 0
OFF_ZB = OFF_ZA + D_MODEL
OFF_QA = OFF_ZB + D_MODEL
OFF_FA = OFF_QA + D_BR
OFF_IA = OFF_FA + D_BR
OFF_GA = OFF_IA + D_BR
OFF_QB = OFF_GA + D_BR
OFF_KB = OFF_QB + D_BR
OFF_VB = OFF_KB + D_BR
OFF_GB = OFF_VB + D_BR
OFF_AB = OFF_GB + D_BR
D_IN_PAD = OFF_AB + RANK_PAD

V7X_VMEM_LIMIT_BYTES = 56 * 1024 * 1024

_F32 = jnp.float32
_BF16 = jnp.bfloat16


def _dot(a, b):
    return jnp.dot(a, b, preferred_element_type=_F32)


def _dot_nt(a, b):
    return lax.dot_general(a, b, (((1,), (1,)), ((), ())), preferred_element_type=_F32)


def _dot_tn(a, b):
    return lax.dot_general(a, b, (((0,), (0,)), ((), ())), preferred_element_type=_F32)


def _rms(x, gain):
    ms = jnp.mean(x * x, axis=-1, keepdims=True)
    return x * lax.rsqrt(ms + NORM_EPS) * gain


def _sigmoid(x):
    return 0.5 * jnp.tanh(0.5 * x) + 0.5


def _silu(x):
    hx = 0.5 * x
    return hx * jnp.tanh(hx) + hx


def _block_diag(a, b):
    zero = jnp.zeros_like(a)
    return jnp.concatenate([jnp.concatenate([a, zero], axis=1), jnp.concatenate([zero, b], axis=1)], axis=0)


def _level_constants():
    t = np.arange(CH)
    mats = []
    for l in range(N_LEVELS):
        s = 1 << l
        ref = (t & ~(2 * s - 1)) + s - 1
        odd = (t & s) != 0
        u = t[None, :]
        a = np.where(odd[:, None], (u > ref[:, None]) & (u <= t[:, None]),
                     (u > t[:, None]) & (u <= ref[:, None]))
        mats.append(a)
    mats.append(t[None, :] <= t[:, None])
    a_all = np.concatenate(mats, axis=0).astype(np.float32)
    a_cat = np.concatenate([a_all, a_all], axis=1)
    x = t[:, None] ^ t[None, :]
    lvl = np.where(x > 0, np.floor(np.log2(np.maximum(x, 1))).astype(np.int32), DIAG_LEVEL)
    lvl = np.where(t[:, None] >= t[None, :], lvl, -1).astype(np.int32)
    return a_cat, np.concatenate([lvl, lvl], axis=1)


def _gla_chunk(q, k, v, g, gate_fn, gain, amat_ref, lvl2, e_ref, qk_ref, w_ref, st_ref, o_ref, rows):
    qk_ref[0] = q
    qk_ref[1] = k
    w_ref[ROW_V] = v.astype(_BF16)
    g2 = g * LOG2_E
    g_hi = g2.astype(_BF16)
    g_lo = (g2 - g_hi.astype(_F32)).astype(_BF16)
    e_ref[...] = _dot(amat_ref[...], jnp.concatenate([g_hi, g_lo], axis=0))

    sub = lax.broadcasted_iota(jnp.int32, (CH, D_HEAD), 0)
    odd_rows = [(sub & (1 << l)) != 0 for l in range(N_LEVELS) if (1 << l) < SUBLANES]
    d_last = []
    for ct in range(N_HEADS):
        cs = slice(ct * D_HEAD, (ct + 1) * D_HEAD)
        qc = qk_ref[0, :, cs]
        kc = qk_ref[1, :, cs]
        for l in range(N_LEVELS):
            s = 1 << l
            if s < SUBLANES:
                src = jnp.where(odd_rows[l], qc, kc)
            else:
                src = jnp.concatenate([(qc if b % 2 else kc)[b * s:(b + 1) * s] for b in range(CH // s)], axis=0)
            w_ref[l, :, cs] = (src * jnp.exp2(e_ref[l * CH:(l + 1) * CH, cs])).astype(_BF16)
        big_g = e_ref[N_LEVELS * CH:(N_LEVELS + 1) * CH, cs]
        g_last = big_g[CH - 1:CH, :]
        w_ref[ROW_Q, :, cs] = qc.astype(_BF16)
        w_ref[ROW_K, :, cs] = kc.astype(_BF16)
        w_ref[ROW_QT, :, cs] = (qc * jnp.exp2(big_g)).astype(_BF16)
        w_ref[ROW_KH, :, cs] = (kc * jnp.exp2(g_last - big_g)).astype(_BF16)
        d_last.append(jnp.exp2(g_last))

    n_groups = CH // SUBLANES
    for pr in range(N_HEADS // 2):
        h0 = slice(pr * D_PAIR, pr * D_PAIR + D_HEAD)
        h1 = slice(pr * D_PAIR + D_HEAD, (pr + 1) * D_PAIR)
        ps = slice(pr * D_PAIR, (pr + 1) * D_PAIR)

        def pair_diag(idx):
            return _block_diag(w_ref[idx, :, h0], w_ref[idx, :, h1])

        groups = [jnp.zeros((SUBLANES, D_PAIR), _F32)] * n_groups

        def merge(level, grp, tile):
            r0 = grp * SUBLANES
            groups[grp] = jnp.where(lvl2[r0:r0 + SUBLANES] == level, tile, groups[grp])

        for l in range(N_LEVELS):
            s = 1 << l
            if s >= BF16_ROWS:
                odd_blocks = [b for b in range(CH // s) if b % 2]
                lhs = jnp.concatenate([w_ref[l, b * s:(b + 1) * s, ps] for b in odd_blocks], axis=0)
                r = _dot_nt(lhs, pair_diag(l))
                for i, b in enumerate(odd_blocks):
                    for gi in range(s // SUBLANES):
                        r0 = i * s + gi * SUBLANES
                        merge(l, b * s // SUBLANES + gi, r[r0:r0 + SUBLANES])
            else:
                r = _dot_nt(w_ref[l, :, ps], pair_diag(l))
                for grp in range(n_groups):
                    if s < SUBLANES or (grp * SUBLANES) & s:
                        merge(l, grp, r[grp * SUBLANES:(grp + 1) * SUBLANES])
        r = _dot_nt(w_ref[ROW_Q, :, ps], pair_diag(ROW_K))
        for grp in range(n_groups):
            merge(DIAG_LEVEL, grp, r[grp * SUBLANES:(grp + 1) * SUBLANES])
        sc = jnp.concatenate(groups, axis=0).astype(_BF16)

        st0 = st_ref[2 * pr]
        st1 = st_ref[2 * pr + 1]
        o = (_dot(sc, pair_diag(ROW_V))
             + _dot_nt(w_ref[ROW_QT, :, ps], _block_diag(st0.astype(_BF16), st1.astype(_BF16))))
        st_ref[2 * pr] = st0 * d_last[2 * pr] + _dot_tn(w_ref[ROW_V, :, h0], w_ref[ROW_KH, :, h0])
        st_ref[2 * pr + 1] = st1 * d_last[2 * pr + 1] + _dot_tn(w_ref[ROW_V, :, h1], w_ref[ROW_KH, :, h1])
        for i, hs in enumerate((h0, h1)):
            o_ref[rows, hs] = _rms(o[:, i * D_HEAD:(i + 1) * D_HEAD], gain[:, hs]) * _silu(gate_fn(hs))


def _block_kernel(x_ref, p_ref, w_in_ref, lb_ref, w_up_ref, b_gla_ref, n_pre_ref, n_post_ref,
                  hn_a_ref, hn_b_ref, w_ba_ref, w_bb_ref, w_out_ref, w_ple_ref, w_pg_ref,
                  n_ple_ref, amat_ref, lvl_ref, out_ref,
                  proj_ref, ea_ref, eb_ref, qka_ref, qkb_ref, wa_ref, wb_ref, st_a_ref, st_b_ref, oa_ref, ob_ref):
    @pl.when(pl.program_id(1) == 0)
    def _():
        st_a_ref[...] = jnp.zeros_like(st_a_ref)
        st_b_ref[...] = jnp.zeros_like(st_b_ref)

    x = x_ref[0]
    proj_ref[...] = _dot(_rms(x, n_pre_ref[...]).astype(_BF16), w_in_ref[...])

    logits = lb_ref[...]
    ex = jnp.exp(logits - jnp.max(logits, axis=0, keepdims=True))
    lb = ex[0:1, :] / jnp.sum(ex, axis=0, keepdims=True)
    fg_half = 0.5 * (1.0 - lb)
    fg_mid = lb + fg_half
    lvl2 = lvl_ref[...]

    def chunk_body(c, carry):
        rows = pl.ds(pl.multiple_of(c * CH, CH), CH)

        def col(off, width=D_BR):
            return proj_ref[rows, off:off + width]

        forget = fg_mid + fg_half * jnp.tanh(0.5 * col(OFF_FA))
        _gla_chunk(_silu(col(OFF_QA)) * (DK_A ** -0.5), 1.0 - forget, col(OFF_IA), jnp.log(forget),
                   lambda hs: proj_ref[rows, OFF_GA + hs.start:OFF_GA + hs.stop], hn_a_ref[...],
                   amat_ref, lvl2, ea_ref, qka_ref, wa_ref, st_a_ref, oa_ref, rows)

        z = _dot(col(OFF_AB, RANK_PAD).astype(_BF16), w_up_ref[...]) + b_gla_ref[...]
        log_alpha = -(jnp.maximum(-z, 0.0) + jnp.log(1.0 + jnp.exp(-jnp.abs(z)))) / GLA_TAU
        _gla_chunk(col(OFF_QB) * (DK_B ** -0.5), col(OFF_KB), col(OFF_VB), log_alpha,
                   lambda hs: proj_ref[rows, OFF_GB + hs.start:OFF_GB + hs.stop], hn_b_ref[...],
                   amat_ref, lvl2, eb_ref, qkb_ref, wb_ref, st_b_ref, ob_ref, rows)
        return carry

    lax.fori_loop(0, TL // CH, chunk_body, 0)

    ya = _dot(oa_ref[...].astype(_BF16), w_ba_ref[...])
    yb = _dot(ob_ref[...].astype(_BF16), w_bb_ref[...])
    y = _sigmoid(proj_ref[:, OFF_ZA:OFF_ZA + D_MODEL]) * ya + _sigmoid(proj_ref[:, OFF_ZB:OFF_ZB + D_MODEL]) * yb
    y = _dot(y.astype(_BF16), w_out_ref[...])
    x1 = x + _rms(y, n_post_ref[...])
    e = _rms(_dot(p_ref[0].astype(_BF16), w_ple_ref[...]), n_ple_ref[...])
    out_ref[0] = x1 + _sigmoid(_dot(x1.astype(_BF16), w_pg_ref[...])) * e


def _pad_heads(w, dk):
    lead = w.shape[:-1]
    w = w.reshape(lead + (N_HEADS, dk))
    w = jnp.pad(w, [(0, 0)] * len(lead) + [(0, 0), (0, D_HEAD - dk)])
    return w.reshape(lead + (N_HEADS * D_HEAD,))


def _const_spec(shape):
    nd = len(shape)
    return pl.BlockSpec(shape, lambda b, t: (0,) * nd, pipeline_mode=pl.Buffered(1))


def kernel(x, p, w_in, lb_logits, w_gla_up, b_gla, norm_pre, norm_post, head_norm_a, head_norm_b,
           w_branch_a, w_branch_b, w_out, w_ple, w_ple_gate, norm_ple):
    bsz, seq, _ = x.shape
    assert x.shape == (bsz, seq, D_MODEL) and seq % TL == 0 and w_in.shape[0] == 1

    wi = w_in[0]
    widths = (D_BR, D_BR, D_BR, D_BR, N_HEADS * DK_B, N_HEADS * DK_B, D_BR, D_BR, GLA_RANK, D_MODEL, D_MODEL)
    splits = [int(s) for s in np.cumsum(widths)[:-1]]
    qa, fa, ia, ga, qb, kb, vb, gb, ab, za, zb = jnp.split(wi, splits, axis=-1)
    ab = jnp.pad(ab, ((0, 0), (0, RANK_PAD - GLA_RANK)))
    w_cat = jnp.concatenate([za, zb, qa, fa, ia, ga, _pad_heads(qb, DK_B), _pad_heads(kb, DK_B), vb, gb, ab],
                            axis=-1).astype(_BF16)
    assert w_cat.shape == (D_MODEL, D_IN_PAD)
    w_up = jnp.pad(_pad_heads(w_gla_up[0], DK_B), ((0, RANK_PAD - GLA_RANK), (0, 0))).astype(_BF16)
    b_up = _pad_heads(b_gla[0], DK_B)[None, :]

    a_cat, lvl2 = _level_constants()
    amat = jnp.asarray(a_cat, dtype=_BF16)
    lvl2 = jnp.asarray(lvl2)

    row = lambda a: a[0][None, :]
    operands = (
        x, p[0], w_cat, lb_logits, w_up, b_up, row(norm_pre), row(norm_post), row(head_norm_a), row(head_norm_b),
        w_branch_a[0].astype(_BF16), w_branch_b[0].astype(_BF16), w_out[0].astype(_BF16),
        w_ple[0].astype(_BF16), w_ple_gate[0].astype(_BF16), row(norm_ple), amat, lvl2,
    )
    in_specs = [
        pl.BlockSpec((1, TL, D_MODEL), lambda b, t: (b, t, 0)),
        pl.BlockSpec((1, TL, D_PLE), lambda b, t: (b, t, 0)),
    ] + [_const_spec(a.shape) for a in operands[2:]]

    return pl.pallas_call(
        _block_kernel,
        grid=(bsz, seq // TL),
        in_specs=in_specs,
        out_specs=pl.BlockSpec((1, TL, D_MODEL), lambda b, t: (b, t, 0)),
        out_shape=jax.ShapeDtypeStruct(x.shape, x.dtype),
        scratch_shapes=[
            pltpu.VMEM((TL, D_IN_PAD), _F32),
            pltpu.VMEM(((N_LEVELS + 1) * CH, D_BR), _F32),
            pltpu.VMEM(((N_LEVELS + 1) * CH, D_BR), _F32),
            pltpu.VMEM((2, CH, D_BR), _F32),
            pltpu.VMEM((2, CH, D_BR), _F32),
            pltpu.VMEM((N_OPERAND_ROWS, CH, D_BR), _BF16),
            pltpu.VMEM((N_OPERAND_ROWS, CH, D_BR), _BF16),
            pltpu.VMEM((N_HEADS, D_HEAD, D_HEAD), _F32),
            pltpu.VMEM((N_HEADS, D_HEAD, D_HEAD), _F32),
            pltpu.VMEM((TL, D_BR), _F32),
            pltpu.VMEM((TL, D_BR), _F32),
        ],
        compiler_params=pltpu.CompilerParams(
            dimension_semantics=("arbitrary", "arbitrary"),
            vmem_limit_bytes=V7X_VMEM_LIMIT_BYTES,
        ),
        name="hgrn2_gla_block",
    )(*operands)
```

```python
import numpy as np
import jax
import jax.numpy as jnp
from jax import lax
from jax.experimental import pallas as pl
from jax.experimental.pallas import tpu as pltpu

D_MODEL = 1024
D_PLE = 256
N_HEADS = 4
D_HEAD = 128
D_BR = N_HEADS * D_HEAD
D_PAIR = 2 * D_HEAD
DK_A = 128
DK_B = 64
GLA_RANK = 16
GLA_TAU = 16.0
NORM_EPS = 1e-6
LOG2_E = 1.4426950408889634

SUBLANES = 8
BF16_ROWS = 16
N_MXU_LEVELS = 3

CH = 128
N_LEVELS = 7
TL = 256
RANK_PAD = 128

ROW_QT = N_LEVELS
ROW_KH = N_LEVELS + 1
ROW_V = N_LEVELS + 2
N_OPERAND_ROWS = N_LEVELS + 3

OFF_ZA = 0
OFF_ZB = OFF_ZA + D_MODEL
OFF_QA = OFF_ZB + D_MODEL
OFF_FA = OFF_QA + D_BR
OFF_IA = OFF_FA + D_BR
OFF_GA = OFF_IA + D_BR
OFF_QB = OFF_GA + D_BR
OFF_KB = OFF_QB + D_BR
OFF_VB = OFF_KB + D_BR
OFF_GB = OFF_VB + D_BR
OFF_AB = OFF_GB + D_BR
D_IN_PAD = OFF_AB + RANK_PAD

V7X_VMEM_LIMIT_BYTES = 56 * 1024 * 1024

_F32 = jnp.float32
_BF16 = jnp.bfloat16


def _dot(a, b):
    return jnp.dot(a, b, preferred_element_type=_F32)


def _dot_nt(a, b):
    return lax.dot_general(a, b, (((1,), (1,)), ((), ())), preferred_element_type=_F32)


def _dot_tn(a, b):
    return lax.dot_general(a, b, (((0,), (0,)), ((), ())), preferred_element_type=_F32)


def _rms(x, gain):
    ms = jnp.mean(x * x, axis=-1, keepdims=True)
    return x * lax.rsqrt(ms + NORM_EPS) * gain


def _sigmoid(x):
    return 0.5 * jnp.tanh(0.5 * x) + 0.5


def _silu(x):
    hx = 0.5 * x
    return hx * jnp.tanh(hx) + hx


def _block_diag(a, b):
    zero = jnp.zeros_like(a)
    return jnp.concatenate([jnp.concatenate([a, zero], axis=1), jnp.concatenate([zero, b], axis=1)], axis=0)


def _level_constants():
    t = np.arange(CH)
    mats = []
    for l in range(N_MXU_LEVELS):
        s = 1 << l
        ref = (t & ~(2 * s - 1)) + s - 1
        odd = (t & s) != 0
        u = t[None, :]
        a = np.where(odd[:, None], (u > ref[:, None]) & (u <= t[:, None]),
                     (u > t[:, None]) & (u <= ref[:, None]))
        mats.append(a)
    mats.append(t[None, :] <= t[:, None])
    a_all = np.concatenate(mats, axis=0).astype(np.float32)
    a_cat = np.concatenate([a_all, a_all], axis=1)
    x = t[:, None] ^ t[None, :]
    lvl = np.where(x > 0, np.floor(np.log2(np.maximum(x, 1))).astype(np.int32), -1)
    lvl = np.where(t[:, None] > t[None, :], lvl, -1).astype(np.int32)
    return a_cat, np.concatenate([lvl, lvl], axis=1)


def _gla_chunk(q, k, v, g, v_fn, gate_fn, gain, amat_ref, lvl2, e_ref, qk_ref, w_ref, st_ref, o_ref, rows):
    qk_ref[0] = q
    qk_ref[1] = k
    w_ref[ROW_V] = v.astype(_BF16)
    g2 = g * LOG2_E
    g_hi = g2.astype(_BF16)
    g_lo = (g2 - g_hi.astype(_F32)).astype(_BF16)
    e_ref[...] = _dot(amat_ref[...], jnp.concatenate([g_hi, g_lo], axis=0))

    sub = lax.broadcasted_iota(jnp.int32, (CH, D_HEAD), 0)
    odd_rows = [(sub & (1 << l)) != 0 for l in range(N_MXU_LEVELS)]
    d_last = []
    for ct in range(N_HEADS):
        cs = slice(ct * D_HEAD, (ct + 1) * D_HEAD)
        qc = qk_ref[0, :, cs]
        kc = qk_ref[1, :, cs]
        big_g = e_ref[N_MXU_LEVELS * CH:(N_MXU_LEVELS + 1) * CH, cs]
        for l in range(N_LEVELS):
            s = 1 << l
            if l < N_MXU_LEVELS:
                w = jnp.where(odd_rows[l], qc, kc) * jnp.exp2(e_ref[l * CH:(l + 1) * CH, cs])
            else:
                parts = []
                for base in range(0, CH, 2 * s):
                    mid = base + s
                    g_ref_row = jnp.broadcast_to(big_g[mid - 1:mid, :], (s, D_HEAD))
                    parts.append(kc[base:mid] * jnp.exp2(g_ref_row - big_g[base:mid]))
                    parts.append(qc[mid:mid + s] * jnp.exp2(big_g[mid:mid + s] - g_ref_row))
                w = jnp.concatenate(parts, axis=0)
            w_ref[l, :, cs] = w.astype(_BF16)
        g_last = big_g[CH - 1:CH, :]
        w_ref[ROW_QT, :, cs] = (qc * jnp.exp2(big_g)).astype(_BF16)
        w_ref[ROW_KH, :, cs] = (kc * jnp.exp2(g_last - big_g)).astype(_BF16)
        d_last.append(jnp.exp2(g_last))

    n_groups = CH // SUBLANES
    for pr in range(N_HEADS // 2):
        h0 = slice(pr * D_PAIR, pr * D_PAIR + D_HEAD)
        h1 = slice(pr * D_PAIR + D_HEAD, (pr + 1) * D_PAIR)
        ps = slice(pr * D_PAIR, (pr + 1) * D_PAIR)

        def keys(idx, lanes, s):
            if s < BF16_ROWS:
                return w_ref[idx, :, lanes]
            zero = jnp.zeros((s, D_HEAD), _BF16)
            return jnp.concatenate([zero if b % 2 else w_ref[idx, b * s:(b + 1) * s, lanes]
                                    for b in range(CH // s)], axis=0)

        groups = [jnp.zeros((SUBLANES, D_PAIR), _F32)] * n_groups

        def merge(level, grp, tile):
            r0 = grp * SUBLANES
            groups[grp] = jnp.where(lvl2[r0:r0 + SUBLANES] == level, tile, groups[grp])

        for l in range(N_LEVELS):
            s = 1 << l
            rhs = _block_diag(keys(l, h0, s), keys(l, h1, s))
            if s >= BF16_ROWS:
                odd_blocks = [b for b in range(CH // s) if b % 2]
                lhs = jnp.concatenate([w_ref[l, b * s:(b + 1) * s, ps] for b in odd_blocks], axis=0)
                r = _dot_nt(lhs, rhs)
                for i, b in enumerate(odd_blocks):
                    for gi in range(s // SUBLANES):
                        r0 = i * s + gi * SUBLANES
                        merge(l, b * s // SUBLANES + gi, r[r0:r0 + SUBLANES])
            else:
                r = _dot_nt(w_ref[l, :, ps], rhs)
                for grp in range(n_groups):
                    if s < SUBLANES or (grp * SUBLANES) & s:
                        merge(l, grp, r[grp * SUBLANES:(grp + 1) * SUBLANES])
        sc = jnp.concatenate(groups, axis=0).astype(_BF16)

        st0 = st_ref[2 * pr]
        st1 = st_ref[2 * pr + 1]
        o = (_dot(sc, _block_diag(w_ref[ROW_V, :, h0], w_ref[ROW_V, :, h1]))
             + _dot_nt(w_ref[ROW_QT, :, ps], _block_diag(st0.astype(_BF16), st1.astype(_BF16))))
        st_ref[2 * pr] = st0 * d_last[2 * pr] + _dot_tn(w_ref[ROW_V, :, h0], w_ref[ROW_KH, :, h0])
        st_ref[2 * pr + 1] = st1 * d_last[2 * pr + 1] + _dot_tn(w_ref[ROW_V, :, h1], w_ref[ROW_KH, :, h1])
        for i, hs in enumerate((h0, h1)):
            self_score = jnp.sum(qk_ref[0, :, hs] * qk_ref[1, :, hs], axis=-1, keepdims=True)
            o_h = o[:, i * D_HEAD:(i + 1) * D_HEAD] + self_score * v_fn(hs)
            o_ref[rows, hs] = _rms(o_h, gain[:, hs]) * _silu(gate_fn(hs))


def _block_kernel(x_ref, p_ref, w_in_ref, lb_ref, w_up_ref, b_gla_ref, n_pre_ref, n_post_ref,
                  hn_a_ref, hn_b_ref, w_ba_ref, w_bb_ref, w_out_ref, w_ple_ref, w_pg_ref,
                  n_ple_ref, amat_ref, lvl_ref, out_ref,
                  proj_ref, ea_ref, eb_ref, qka_ref, qkb_ref, wa_ref, wb_ref, st_a_ref, st_b_ref, oa_ref, ob_ref):
    @pl.when(pl.program_id(1) == 0)
    def _():
        st_a_ref[...] = jnp.zeros_like(st_a_ref)
        st_b_ref[...] = jnp.zeros_like(st_b_ref)

    x = x_ref[0]
    proj_ref[...] = _dot(_rms(x, n_pre_ref[...]).astype(_BF16), w_in_ref[...])

    logits = lb_ref[...]
    ex = jnp.exp(logits - jnp.max(logits, axis=0, keepdims=True))
    lb = ex[0:1, :] / jnp.sum(ex, axis=0, keepdims=True)
    fg_half = 0.5 * (1.0 - lb)
    fg_mid = lb + fg_half
    lvl2 = lvl_ref[...]

    def chunk_body(c, carry):
        rows = pl.ds(pl.multiple_of(c * CH, CH), CH)

        def col(off, width=D_BR):
            return proj_ref[rows, off:off + width]

        forget = fg_mid + fg_half * jnp.tanh(0.5 * col(OFF_FA))
        _gla_chunk(_silu(col(OFF_QA)) * (DK_A ** -0.5), 1.0 - forget, col(OFF_IA), jnp.log(forget),
                   lambda hs: proj_ref[rows, OFF_IA + hs.start:OFF_IA + hs.stop],
                   lambda hs: proj_ref[rows, OFF_GA + hs.start:OFF_GA + hs.stop], hn_a_ref[...],
                   amat_ref, lvl2, ea_ref, qka_ref, wa_ref, st_a_ref, oa_ref, rows)

        z = _dot(col(OFF_AB, RANK_PAD).astype(_BF16), w_up_ref[...]) + b_gla_ref[...]
        log_alpha = -(jnp.maximum(-z, 0.0) + jnp.log(1.0 + jnp.exp(-jnp.abs(z)))) / GLA_TAU
        _gla_chunk(col(OFF_QB) * (DK_B ** -0.5), col(OFF_KB), col(OFF_VB), log_alpha,
                   lambda hs: proj_ref[rows, OFF_VB + hs.start:OFF_VB + hs.stop],
                   lambda hs: proj_ref[rows, OFF_GB + hs.start:OFF_GB + hs.stop], hn_b_ref[...],
                   amat_ref, lvl2, eb_ref, qkb_ref, wb_ref, st_b_ref, ob_ref, rows)
        return carry

    lax.fori_loop(0, TL // CH, chunk_body, 0)

    ya = _dot(oa_ref[...].astype(_BF16), w_ba_ref[...])
    yb = _dot(ob_ref[...].astype(_BF16), w_bb_ref[...])
    y = _sigmoid(proj_ref[:, OFF_ZA:OFF_ZA + D_MODEL]) * ya + _sigmoid(proj_ref[:, OFF_ZB:OFF_ZB + D_MODEL]) * yb
    y = _dot(y.astype(_BF16), w_out_ref[...])
    x1 = x + _rms(y, n_post_ref[...])
    e = _rms(_dot(p_ref[0, 0].astype(_BF16), w_ple_ref[...]), n_ple_ref[...])
    out_ref[0] = x1 + _sigmoid(_dot(x1.astype(_BF16), w_pg_ref[...])) * e


def _pad_heads(w, dk):
    lead = w.shape[:-1]
    w = w.reshape(lead + (N_HEADS, dk))
    w = jnp.pad(w, [(0, 0)] * len(lead) + [(0, 0), (0, D_HEAD - dk)])
    return w.reshape(lead + (N_HEADS * D_HEAD,))


def _const_spec(shape):
    nd = len(shape)
    return pl.BlockSpec(shape, lambda b, t: (0,) * nd, pipeline_mode=pl.Buffered(1))


def kernel(x, p, w_in, lb_logits, w_gla_up, b_gla, norm_pre, norm_post, head_norm_a, head_norm_b,
           w_branch_a, w_branch_b, w_out, w_ple, w_ple_gate, norm_ple):
    bsz, seq, _ = x.shape
    assert x.shape == (bsz, seq, D_MODEL) and seq % TL == 0 and w_in.shape[0] == 1 and p.shape[0] == 1

    wi = w_in[0]
    widths = (D_BR, D_BR, D_BR, D_BR, N_HEADS * DK_B, N_HEADS * DK_B, D_BR, D_BR, GLA_RANK, D_MODEL, D_MODEL)
    splits = [int(s) for s in np.cumsum(widths)[:-1]]
    qa, fa, ia, ga, qb, kb, vb, gb, ab, za, zb = jnp.split(wi, splits, axis=-1)
    ab = jnp.pad(ab, ((0, 0), (0, RANK_PAD - GLA_RANK)))
    w_cat = jnp.concatenate([za, zb, qa, fa, ia, ga, _pad_heads(qb, DK_B), _pad_heads(kb, DK_B), vb, gb, ab],
                            axis=-1).astype(_BF16)
    assert w_cat.shape == (D_MODEL, D_IN_PAD)
    w_up = jnp.pad(_pad_heads(w_gla_up[0], DK_B), ((0, RANK_PAD - GLA_RANK), (0, 0))).astype(_BF16)
    b_up = _pad_heads(b_gla[0], DK_B)[None, :]

    a_cat, lvl2 = _level_constants()
    amat = jnp.asarray(a_cat, dtype=_BF16)
    lvl2 = jnp.asarray(lvl2)

    row = lambda a: a[0][None, :]
    operands = (
        x, p, w_cat, lb_logits, w_up, b_up, row(norm_pre), row(norm_post), row(head_norm_a), row(head_norm_b),
        w_branch_a[0].astype(_BF16), w_branch_b[0].astype(_BF16), w_out[0].astype(_BF16),
        w_ple[0].astype(_BF16), w_ple_gate[0].astype(_BF16), row(norm_ple), amat, lvl2,
    )
    in_specs = [
        pl.BlockSpec((1, TL, D_MODEL), lambda b, t: (b, t, 0)),
        pl.BlockSpec((1, 1, TL, D_PLE), lambda b, t: (0, b, t, 0)),
    ] + [_const_spec(a.shape) for a in operands[2:]]

    return pl.pallas_call(
        _block_kernel,
        grid=(bsz, seq // TL),
        in_specs=in_specs,
        out_specs=pl.BlockSpec((1, TL, D_MODEL), lambda b, t: (b, t, 0)),
        out_shape=jax.ShapeDtypeStruct(x.shape, x.dtype),
        scratch_shapes=[
            pltpu.VMEM((TL, D_IN_PAD), _F32),
            pltpu.VMEM(((N_MXU_LEVELS + 1) * CH, D_BR), _F32),
            pltpu.VMEM(((N_MXU_LEVELS + 1) * CH, D_BR), _F32),
            pltpu.VMEM((2, CH, D_BR), _F32),
            pltpu.VMEM((2, CH, D_BR), _F32),
            pltpu.VMEM((N_OPERAND_ROWS, CH, D_BR), _BF16),
            pltpu.VMEM((N_OPERAND_ROWS, CH, D_BR), _BF16),
            pltpu.VMEM((N_HEADS, D_HEAD, D_HEAD), _F32),
            pltpu.VMEM((N_HEADS, D_HEAD, D_HEAD), _F32),
            pltpu.VMEM((TL, D_BR), _F32),
            pltpu.VMEM((TL, D_BR), _F32),
        ],
        compiler_params=pltpu.CompilerParams(
            dimension_semantics=("arbitrary", "arbitrary"),
            vmem_limit_bytes=V7X_VMEM_LIMIT_BYTES,
        ),
        name="hgrn2_gla_block",
    )(*operands)
```

```python
import numpy as np
import jax
import jax.numpy as jnp
from jax import lax
from jax.experimental import pallas as pl
from jax.experimental.pallas import tpu as pltpu

D_MODEL = 1024
D_PLE = 256
N_HEADS = 4
D_HEAD = 128
D_BR = N_HEADS * D_HEAD
D_PAIR = 2 * D_HEAD
DK_A = 128
DK_B = 64
GLA_RANK = 16
GLA_TAU = 16.0
NORM_EPS = 1e-6
LOG2_E = 1.4426950408889634

SUBLANES = 8
BF16_ROWS = 16
N_MXU_LEVELS = 3

CH = 128
N_LEVELS = 7
TL = 512
RANK_PAD = 128

ROW_QT = N_LEVELS
ROW_KH = N_LEVELS + 1
ROW_V = N_LEVELS + 2
N_OPERAND_ROWS = N_LEVELS + 3

OFF_ZA = 0
OFF_ZB = OFF_ZA + D_MODEL
OFF_QA = OFF_ZB + D_MODEL
OFF_FA = OFF_QA + D_BR
OFF_IA = OFF_FA + D_BR
OFF_GA = OFF_IA + D_BR
OFF_QB = OFF_GA + D_BR
OFF_KB = OFF_QB + D_BR
OFF_VB = OFF_KB + D_BR
OFF_GB = OFF_VB + D_BR
OFF_AB = OFF_GB + D_BR
D_IN_PAD = OFF_AB + RANK_PAD

V7X_VMEM_LIMIT_BYTES = 56 * 1024 * 1024

_F32 = jnp.float32
_BF16 = jnp.bfloat16


def _dot(a, b):
    return jnp.dot(a, b, preferred_element_type=_F32)


def _dot_nt(a, b):
    return lax.dot_general(a, b, (((1,), (1,)), ((), ())), preferred_element_type=_F32)


def _dot_tn(a, b):
    return lax.dot_general(a, b, (((0,), (0,)), ((), ())), preferred_element_type=_F32)


def _rms(x, gain):
    ms = jnp.mean(x * x, axis=-1, keepdims=True)
    return x * lax.rsqrt(ms + NORM_EPS) * gain


def _sigmoid(x):
    return 0.5 * jnp.tanh(0.5 * x) + 0.5


def _silu(x):
    hx = 0.5 * x
    return hx * jnp.tanh(hx) + hx


def _block_diag(a, b):
    zero = jnp.zeros_like(a)
    return jnp.concatenate([jnp.concatenate([a, zero], axis=1), jnp.concatenate([zero, b], axis=1)], axis=0)


def _level_constants():
    t = np.arange(CH)
    mats = []
    for l in range(N_MXU_LEVELS):
        s = 1 << l
        ref = (t & ~(2 * s - 1)) + s - 1
        odd = (t & s) != 0
        u = t[None, :]
        a = np.where(odd[:, None], (u > ref[:, None]) & (u <= t[:, None]),
                     (u > t[:, None]) & (u <= ref[:, None]))
        mats.append(a)
    mats.append(t[None, :] <= t[:, None])
    a_all = np.concatenate(mats, axis=0).astype(np.float32)
    a_cat = np.concatenate([a_all, a_all], axis=1)
    x = t[:, None] ^ t[None, :]
    lvl = np.where(x > 0, np.floor(np.log2(np.maximum(x, 1))).astype(np.int32), -1)
    lvl = np.where(t[:, None] > t[None, :], lvl, -1).astype(np.int32)
    return a_cat, np.concatenate([lvl, lvl], axis=1)


def _gla_chunk(q, k, v, g, v_fn, gate_fn, gain, amat_ref, lvl2, e_ref, qk_ref, w_ref, st_ref, o_ref, rows):
    qk_ref[0] = q
    qk_ref[1] = k
    w_ref[ROW_V] = v.astype(_BF16)
    g2 = g * LOG2_E
    g_hi = g2.astype(_BF16)
    g_lo = (g2 - g_hi.astype(_F32)).astype(_BF16)
    e_ref[...] = _dot(amat_ref[...], jnp.concatenate([g_hi, g_lo], axis=0))

    sub = lax.broadcasted_iota(jnp.int32, (CH, D_HEAD), 0)
    odd_rows = [(sub & (1 << l)) != 0 for l in range(N_MXU_LEVELS)]
    d_last = []
    for ct in range(N_HEADS):
        cs = slice(ct * D_HEAD, (ct + 1) * D_HEAD)
        qc = qk_ref[0, :, cs]
        kc = qk_ref[1, :, cs]
        big_g = e_ref[N_MXU_LEVELS * CH:(N_MXU_LEVELS + 1) * CH, cs]
        for l in range(N_LEVELS):
            s = 1 << l
            if l < N_MXU_LEVELS:
                w = jnp.where(odd_rows[l], qc, kc) * jnp.exp2(e_ref[l * CH:(l + 1) * CH, cs])
            else:
                parts = []
                for base in range(0, CH, 2 * s):
                    mid = base + s
                    g_ref_row = jnp.broadcast_to(big_g[mid - 1:mid, :], (s, D_HEAD))
                    parts.append(kc[base:mid] * jnp.exp2(g_ref_row - big_g[base:mid]))
                    parts.append(qc[mid:mid + s] * jnp.exp2(big_g[mid:mid + s] - g_ref_row))
                w = jnp.concatenate(parts, axis=0)
            w_ref[l, :, cs] = w.astype(_BF16)
        g_last = big_g[CH - 1:CH, :]
        w_ref[ROW_QT, :, cs] = (qc * jnp.exp2(big_g)).astype(_BF16)
        w_ref[ROW_KH, :, cs] = (kc * jnp.exp2(g_last - big_g)).astype(_BF16)
        d_last.append(jnp.exp2(g_last))

    n_groups = CH // SUBLANES
    for pr in range(N_HEADS // 2):
        h0 = slice(pr * D_PAIR, pr * D_PAIR + D_HEAD)
        h1 = slice(pr * D_PAIR + D_HEAD, (pr + 1) * D_PAIR)
        ps = slice(pr * D_PAIR, (pr + 1) * D_PAIR)

        def keys(idx, lanes, s):
            if s < BF16_ROWS:
                return w_ref[idx, :, lanes]
            zero = jnp.zeros((s, D_HEAD), _BF16)
            return jnp.concatenate([zero if b % 2 else w_ref[idx, b * s:(b + 1) * s, lanes]
                                    for b in range(CH // s)], axis=0)

        groups = [jnp.zeros((SUBLANES, D_PAIR), _F32)] * n_groups

        def merge(level, grp, tile):
            r0 = grp * SUBLANES
            groups[grp] = jnp.where(lvl2[r0:r0 + SUBLANES] == level, tile, groups[grp])

        for l in range(N_LEVELS):
            s = 1 << l
            rhs = _block_diag(keys(l, h0, s), keys(l, h1, s))
            if s >= BF16_ROWS:
                odd_blocks = [b for b in range(CH // s) if b % 2]
                lhs = jnp.concatenate([w_ref[l, b * s:(b + 1) * s, ps] for b in odd_blocks], axis=0)
                r = _dot_nt(lhs, rhs)
                for i, b in enumerate(odd_blocks):
                    for gi in range(s // SUBLANES):
                        r0 = i * s + gi * SUBLANES
                        merge(l, b * s // SUBLANES + gi, r[r0:r0 + SUBLANES])
            else:
                r = _dot_nt(w_ref[l, :, ps], rhs)
                for grp in range(n_groups):
                    if s < SUBLANES or (grp * SUBLANES) & s:
                        merge(l, grp, r[grp * SUBLANES:(grp + 1) * SUBLANES])
        sc = jnp.concatenate(groups, axis=0).astype(_BF16)

        st0 = st_ref[2 * pr]
        st1 = st_ref[2 * pr + 1]
        o = (_dot(sc, _block_diag(w_ref[ROW_V, :, h0], w_ref[ROW_V, :, h1]))
             + _dot_nt(w_ref[ROW_QT, :, ps], _block_diag(st0.astype(_BF16), st1.astype(_BF16))))
        st_ref[2 * pr] = st0 * d_last[2 * pr] + _dot_tn(w_ref[ROW_V, :, h0], w_ref[ROW_KH, :, h0])
        st_ref[2 * pr + 1] = st1 * d_last[2 * pr + 1] + _dot_tn(w_ref[ROW_V, :, h1], w_ref[ROW_KH, :, h1])
        for i, hs in enumerate((h0, h1)):
            self_score = jnp.sum(qk_ref[0, :, hs] * qk_ref[1, :, hs], axis=-1, keepdims=True)
            o_h = o[:, i * D_HEAD:(i + 1) * D_HEAD] + self_score * v_fn(hs)
            o_ref[rows, hs] = _rms(o_h, gain[:, hs]) * _silu(gate_fn(hs))


def _block_kernel(x_ref, p_ref, w_in_ref, lb_ref, w_up_ref, b_gla_ref, n_pre_ref, n_post_ref,
                  hn_a_ref, hn_b_ref, w_ba_ref, w_bb_ref, w_out_ref, w_ple_ref, w_pg_ref,
                  n_ple_ref, amat_ref, lvl_ref, out_ref,
                  proj_ref, ea_ref, eb_ref, qka_ref, qkb_ref, wa_ref, wb_ref, st_a_ref, st_b_ref, oa_ref, ob_ref):
    @pl.when(pl.program_id(1) == 0)
    def _():
        st_a_ref[...] = jnp.zeros_like(st_a_ref)
        st_b_ref[...] = jnp.zeros_like(st_b_ref)

    x = x_ref[0]
    proj_ref[...] = _dot(_rms(x, n_pre_ref[...]).astype(_BF16), w_in_ref[...])

    logits = lb_ref[...]
    ex = jnp.exp(logits - jnp.max(logits, axis=0, keepdims=True))
    lb = ex[0:1, :] / jnp.sum(ex, axis=0, keepdims=True)
    fg_half = 0.5 * (1.0 - lb)
    fg_mid = lb + fg_half
    lvl2 = lvl_ref[...]

    def chunk_body(c, carry):
        rows = pl.ds(pl.multiple_of(c * CH, CH), CH)

        def col(off, width=D_BR):
            return proj_ref[rows, off:off + width]

        forget = fg_mid + fg_half * jnp.tanh(0.5 * col(OFF_FA))
        _gla_chunk(_silu(col(OFF_QA)) * (DK_A ** -0.5), 1.0 - forget, col(OFF_IA), jnp.log(forget),
                   lambda hs: proj_ref[rows, OFF_IA + hs.start:OFF_IA + hs.stop],
                   lambda hs: proj_ref[rows, OFF_GA + hs.start:OFF_GA + hs.stop], hn_a_ref[...],
                   amat_ref, lvl2, ea_ref, qka_ref, wa_ref, st_a_ref, oa_ref, rows)

        z = _dot(col(OFF_AB, RANK_PAD).astype(_BF16), w_up_ref[...]) + b_gla_ref[...]
        log_alpha = -(jnp.maximum(-z, 0.0) + jnp.log(1.0 + jnp.exp(-jnp.abs(z)))) / GLA_TAU
        _gla_chunk(col(OFF_QB) * (DK_B ** -0.5), col(OFF_KB), col(OFF_VB), log_alpha,
                   lambda hs: proj_ref[rows, OFF_VB + hs.start:OFF_VB + hs.stop],
                   lambda hs: proj_ref[rows, OFF_GB + hs.start:OFF_GB + hs.stop], hn_b_ref[...],
                   amat_ref, lvl2, eb_ref, qkb_ref, wb_ref, st_b_ref, ob_ref, rows)
        return carry

    lax.fori_loop(0, TL // CH, chunk_body, 0)

    ya = _dot(oa_ref[...].astype(_BF16), w_ba_ref[...])
    yb = _dot(ob_ref[...].astype(_BF16), w_bb_ref[...])
    y = _sigmoid(proj_ref[:, OFF_ZA:OFF_ZA + D_MODEL]) * ya + _sigmoid(proj_ref[:, OFF_ZB:OFF_ZB + D_MODEL]) * yb
    y = _dot(y.astype(_BF16), w_out_ref[...])
    x1 = x + _rms(y, n_post_ref[...])
    e = _rms(_dot(p_ref[0, 0].astype(_BF16), w_ple_ref[...]), n_ple_ref[...])
    out_ref[0] = x1 + _sigmoid(_dot(x1.astype(_BF16), w_pg_ref[...])) * e


def _pad_heads(w, dk):
    lead = w.shape[:-1]
    w = w.reshape(lead + (N_HEADS, dk))
    w = jnp.pad(w, [(0, 0)] * len(lead) + [(0, 0), (0, D_HEAD - dk)])
    return w.reshape(lead + (N_HEADS * D_HEAD,))


def _const_spec(shape):
    nd = len(shape)
    return pl.BlockSpec(shape, lambda b, t: (0,) * nd, pipeline_mode=pl.Buffered(1))


def kernel(x, p, w_in, lb_logits, w_gla_up, b_gla, norm_pre, norm_post, head_norm_a, head_norm_b,
           w_branch_a, w_branch_b, w_out, w_ple, w_ple_gate, norm_ple):
    bsz, seq, _ = x.shape
    assert x.shape == (bsz, seq, D_MODEL) and seq % TL == 0 and w_in.shape[0] == 1 and p.shape[0] == 1

    wi = w_in[0]
    widths = (D_BR, D_BR, D_BR, D_BR, N_HEADS * DK_B, N_HEADS * DK_B, D_BR, D_BR, GLA_RANK, D_MODEL, D_MODEL)
    splits = [int(s) for s in np.cumsum(widths)[:-1]]
    qa, fa, ia, ga, qb, kb, vb, gb, ab, za, zb = jnp.split(wi, splits, axis=-1)
    ab = jnp.pad(ab, ((0, 0), (0, RANK_PAD - GLA_RANK)))
    w_cat = jnp.concatenate([za, zb, qa, fa, ia, ga, _pad_heads(qb, DK_B), _pad_heads(kb, DK_B), vb, gb, ab],
                            axis=-1).astype(_BF16)
    assert w_cat.shape == (D_MODEL, D_IN_PAD)
    w_up = jnp.pad(_pad_heads(w_gla_up[0], DK_B), ((0, RANK_PAD - GLA_RANK), (0, 0))).astype(_BF16)
    b_up = _pad_heads(b_gla[0], DK_B)[None, :]

    a_cat, lvl2 = _level_constants()
    amat = jnp.asarray(a_cat, dtype=_BF16)
    lvl2 = jnp.asarray(lvl2)

    row = lambda a: a[0][None, :]
    operands = (
        x, p, w_cat, lb_logits, w_up, b_up, row(norm_pre), row(norm_post), row(head_norm_a), row(head_norm_b),
        w_branch_a[0].astype(_BF16), w_branch_b[0].astype(_BF16), w_out[0].astype(_BF16),
        w_ple[0].astype(_BF16), w_ple_gate[0].astype(_BF16), row(norm_ple), amat, lvl2,
    )
    in_specs = [
        pl.BlockSpec((1, TL, D_MODEL), lambda b, t: (b, t, 0)),
        pl.BlockSpec((1, 1, TL, D_PLE), lambda b, t: (0, b, t, 0)),
    ] + [_const_spec(a.shape) for a in operands[2:]]

    return pl.pallas_call(
        _block_kernel,
        grid=(bsz, seq // TL),
        in_specs=in_specs,
        out_specs=pl.BlockSpec((1, TL, D_MODEL), lambda b, t: (b, t, 0)),
        out_shape=jax.ShapeDtypeStruct(x.shape, x.dtype),
        scratch_shapes=[
            pltpu.VMEM((TL, D_IN_PAD), _F32),
            pltpu.VMEM(((N_MXU_LEVELS + 1) * CH, D_BR), _F32),
            pltpu.VMEM(((N_MXU_LEVELS + 1) * CH, D_BR), _F32),
            pltpu.VMEM((2, CH, D_BR), _F32),
            pltpu.VMEM((2, CH, D_BR), _F32),
            pltpu.VMEM((N_OPERAND_ROWS, CH, D_BR), _BF16),
            pltpu.VMEM((N_OPERAND_ROWS, CH, D_BR), _BF16),
            pltpu.VMEM((N_HEADS, D_HEAD, D_HEAD), _F32),
            pltpu.VMEM((N_HEADS, D_HEAD, D_HEAD), _F32),
            pltpu.VMEM((TL, D_BR), _F32),
            pltpu.VMEM((TL, D_BR), _F32),
        ],
        compiler_params=pltpu.CompilerParams(
            dimension_semantics=("arbitrary", "arbitrary"),
            vmem_limit_bytes=V7X_VMEM_LIMIT_BYTES,
        ),
        name="hgrn2_gla_block",
    )(*operands)
```

```python
import numpy as np
import jax
import jax.numpy as jnp
from jax import lax
from jax.experimental import pallas as pl
from jax.experimental.pallas import tpu as pltpu

D_MODEL = 1024
D_PLE = 256
N_HEADS = 4
D_HEAD = 128
D_BR = N_HEADS * D_HEAD
D_PAIR = 2 * D_HEAD
DK_A = 128
DK_B = 64
GLA_RANK = 16
GLA_TAU = 16.0
NORM_EPS = 1e-6
LOG2_E = 1.4426950408889634

SUBLANES = 8
BF16_ROWS = 16
N_MXU_LEVELS = 3

CH = 128
N_LEVELS = 7
TL = 512
RANK_PAD = 128

ROW_QT = N_LEVELS
ROW_KH = N_LEVELS + 1
ROW_V = N_LEVELS + 2
N_OPERAND_ROWS = N_LEVELS + 3

OFF_ZA = 0
OFF_ZB = OFF_ZA + D_MODEL
OFF_QA = OFF_ZB + D_MODEL
OFF_FA = OFF_QA + D_BR
OFF_IA = OFF_FA + D_BR
OFF_GA = OFF_IA + D_BR
OFF_QB = OFF_GA + D_BR
OFF_KB = OFF_QB + D_BR
OFF_VB = OFF_KB + D_BR
OFF_GB = OFF_VB + D_BR
OFF_AB = OFF_GB + D_BR
D_IN_PAD = OFF_AB + RANK_PAD

V7X_VMEM_LIMIT_BYTES = 56 * 1024 * 1024

_F32 = jnp.float32
_BF16 = jnp.bfloat16


def _dot(a, b):
    return jnp.dot(a, b, preferred_element_type=_F32)


def _dot_nt(a, b):
    return lax.dot_general(a, b, (((1,), (1,)), ((), ())), preferred_element_type=_F32)


def _dot_tn(a, b):
    return lax.dot_general(a, b, (((0,), (0,)), ((), ())), preferred_element_type=_F32)


def _rms(x, gain):
    ms = jnp.mean(x * x, axis=-1, keepdims=True)
    return x * lax.rsqrt(ms + NORM_EPS) * gain


def _sigmoid(x):
    return 0.5 * jnp.tanh(0.5 * x) + 0.5


def _silu(x):
    hx = 0.5 * x
    return hx * jnp.tanh(hx) + hx


def _block_diag(a, b):
    zero = jnp.zeros_like(a)
    return jnp.concatenate([jnp.concatenate([a, zero], axis=1), jnp.concatenate([zero, b], axis=1)], axis=0)


def _level_constants():
    t = np.arange(CH)
    mats = []
    for l in range(N_MXU_LEVELS):
        s = 1 << l
        ref = (t & ~(2 * s - 1)) + s - 1
        odd = (t & s) != 0
        u = t[None, :]
        a = np.where(odd[:, None], (u > ref[:, None]) & (u <= t[:, None]),
                     (u > t[:, None]) & (u <= ref[:, None]))
        mats.append(a)
    mats.append(t[None, :] <= t[:, None])
    a_all = np.concatenate(mats, axis=0).astype(np.float32)
    a_cat = np.concatenate([a_all, a_all], axis=1)
    x = t[:, None] ^ t[None, :]
    lvl = np.where(x > 0, np.floor(np.log2(np.maximum(x, 1))).astype(np.int32), -1)
    lvl = np.where(t[:, None] > t[None, :], lvl, -1).astype(np.int32)
    return a_cat, np.concatenate([lvl, lvl], axis=1)


def _gla_chunk(q, k, v, g, v_fn, gate_fn, gain, amat_ref, lvl2, e_ref, qk_ref, w_ref, sc_ref, st_ref, o_ref, rows):
    qk_ref[0] = q
    qk_ref[1] = k
    w_ref[ROW_V] = v.astype(_BF16)
    g2 = g * LOG2_E
    g_hi = g2.astype(_BF16)
    g_lo = (g2 - g_hi.astype(_F32)).astype(_BF16)
    e_ref[...] = _dot(amat_ref[...], jnp.concatenate([g_hi, g_lo], axis=0))
    yield

    sub = lax.broadcasted_iota(jnp.int32, (CH, D_HEAD), 0)
    odd_rows = [(sub & (1 << l)) != 0 for l in range(N_MXU_LEVELS)]
    d_last = []
    for ct in range(N_HEADS):
        cs = slice(ct * D_HEAD, (ct + 1) * D_HEAD)
        qc = qk_ref[0, :, cs]
        kc = qk_ref[1, :, cs]
        big_g = e_ref[N_MXU_LEVELS * CH:(N_MXU_LEVELS + 1) * CH, cs]
        for l in range(N_LEVELS):
            s = 1 << l
            if l < N_MXU_LEVELS:
                w = jnp.where(odd_rows[l], qc, kc) * jnp.exp2(e_ref[l * CH:(l + 1) * CH, cs])
            else:
                parts = []
                for base in range(0, CH, 2 * s):
                    mid = base + s
                    g_ref_row = jnp.broadcast_to(big_g[mid - 1:mid, :], (s, D_HEAD))
                    parts.append(kc[base:mid] * jnp.exp2(g_ref_row - big_g[base:mid]))
                    parts.append(qc[mid:mid + s] * jnp.exp2(big_g[mid:mid + s] - g_ref_row))
                w = jnp.concatenate(parts, axis=0)
            w_ref[l, :, cs] = w.astype(_BF16)
        g_last = big_g[CH - 1:CH, :]
        w_ref[ROW_QT, :, cs] = (qc * jnp.exp2(big_g)).astype(_BF16)
        w_ref[ROW_KH, :, cs] = (kc * jnp.exp2(g_last - big_g)).astype(_BF16)
        d_last.append(jnp.exp2(g_last))
        yield

    for pr in range(N_HEADS // 2):
        h0 = slice(pr * D_PAIR, pr * D_PAIR + D_HEAD)
        h1 = slice(pr * D_PAIR + D_HEAD, (pr + 1) * D_PAIR)
        ps = slice(pr * D_PAIR, (pr + 1) * D_PAIR)

        def keys(idx, lanes, s):
            if s < BF16_ROWS:
                return w_ref[idx, :, lanes]
            zero = jnp.zeros((s, D_HEAD), _BF16)
            return jnp.concatenate([zero if b % 2 else w_ref[idx, b * s:(b + 1) * s, lanes]
                                    for b in range(CH // s)], axis=0)

        low = jnp.zeros((CH, D_PAIR), _F32)
        for l in range(N_LEVELS):
            s = 1 << l
            rhs = _block_diag(keys(l, h0, s), keys(l, h1, s))
            odd_blocks = [b for b in range(CH // s) if b % 2]
            if s >= BF16_ROWS:
                lhs = jnp.concatenate([w_ref[l, b * s:(b + 1) * s, ps] for b in odd_blocks], axis=0)
            else:
                lhs = w_ref[l, :, ps]
            r = _dot_nt(lhs, rhs)
            if s < SUBLANES:
                low = jnp.where(lvl2 == l, r, low)
                if 2 * s == SUBLANES:
                    sc_ref[pr] = low
                continue
            for i, b in enumerate(odd_blocks):
                src_row = i * s if s >= BF16_ROWS else b * s
                for head in range(2):
                    c0 = head * D_HEAD + (b - 1) * s
                    sc_ref[pr, b * s:(b + 1) * s, c0:c0 + s] = r[src_row:src_row + s, c0:c0 + s]
        sc = sc_ref[pr].astype(_BF16)

        st0 = st_ref[2 * pr]
        st1 = st_ref[2 * pr + 1]
        o = (_dot(sc, _block_diag(w_ref[ROW_V, :, h0], w_ref[ROW_V, :, h1]))
             + _dot_nt(w_ref[ROW_QT, :, ps], _block_diag(st0.astype(_BF16), st1.astype(_BF16))))
        st_ref[2 * pr] = st0 * d_last[2 * pr] + _dot_tn(w_ref[ROW_V, :, h0], w_ref[ROW_KH, :, h0])
        st_ref[2 * pr + 1] = st1 * d_last[2 * pr + 1] + _dot_tn(w_ref[ROW_V, :, h1], w_ref[ROW_KH, :, h1])
        for i, hs in enumerate((h0, h1)):
            self_score = jnp.sum(qk_ref[0, :, hs] * qk_ref[1, :, hs], axis=-1, keepdims=True)
            o_h = o[:, i * D_HEAD:(i + 1) * D_HEAD] + self_score * v_fn(hs)
            o_ref[rows, hs] = _rms(o_h, gain[:, hs]) * _silu(gate_fn(hs))
        yield


def _block_kernel(x_ref, p_ref, w_in_ref, lb_ref, w_up_ref, b_gla_ref, n_pre_ref, n_post_ref,
                  hn_a_ref, hn_b_ref, w_ba_ref, w_bb_ref, w_out_ref, w_ple_ref, w_pg_ref,
                  n_ple_ref, amat_ref, lvl_ref, out_ref,
                  proj_ref, ea_ref, eb_ref, qka_ref, qkb_ref, wa_ref, wb_ref, sca_ref, scb_ref, st_a_ref, st_b_ref, oa_ref, ob_ref):
    @pl.when(pl.program_id(1) == 0)
    def _():
        st_a_ref[...] = jnp.zeros_like(st_a_ref)
        st_b_ref[...] = jnp.zeros_like(st_b_ref)

    x = x_ref[0]
    proj_ref[...] = _dot(_rms(x, n_pre_ref[...]).astype(_BF16), w_in_ref[...])

    logits = lb_ref[...]
    ex = jnp.exp(logits - jnp.max(logits, axis=0, keepdims=True))
    lb = ex[0:1, :] / jnp.sum(ex, axis=0, keepdims=True)
    fg_half = 0.5 * (1.0 - lb)
    fg_mid = lb + fg_half
    lvl2 = lvl_ref[...]

    def chunk_body(c, carry):
        rows = pl.ds(pl.multiple_of(c * CH, CH), CH)

        def col(off, width=D_BR):
            return proj_ref[rows, off:off + width]

        forget = fg_mid + fg_half * jnp.tanh(0.5 * col(OFF_FA))
        branch_a = _gla_chunk(_silu(col(OFF_QA)) * (DK_A ** -0.5), 1.0 - forget, col(OFF_IA), jnp.log(forget),
                   lambda hs: proj_ref[rows, OFF_IA + hs.start:OFF_IA + hs.stop],
                   lambda hs: proj_ref[rows, OFF_GA + hs.start:OFF_GA + hs.stop], hn_a_ref[...],
                   amat_ref, lvl2, ea_ref, qka_ref, wa_ref, sca_ref, st_a_ref, oa_ref, rows)

        z = _dot(col(OFF_AB, RANK_PAD).astype(_BF16), w_up_ref[...]) + b_gla_ref[...]
        log_alpha = -(jnp.maximum(-z, 0.0) + jnp.log(1.0 + jnp.exp(-jnp.abs(z)))) / GLA_TAU
        branch_b = _gla_chunk(col(OFF_QB) * (DK_B ** -0.5), col(OFF_KB), col(OFF_VB), log_alpha,
                   lambda hs: proj_ref[rows, OFF_VB + hs.start:OFF_VB + hs.stop],
                   lambda hs: proj_ref[rows, OFF_GB + hs.start:OFF_GB + hs.stop], hn_b_ref[...],
                   amat_ref, lvl2, eb_ref, qkb_ref, wb_ref, scb_ref, st_b_ref, ob_ref, rows)
        for _ in zip(branch_a, branch_b):
            pass
        return carry

    lax.fori_loop(0, TL // CH, chunk_body, 0)

    ya = _dot(oa_ref[...].astype(_BF16), w_ba_ref[...])
    yb = _dot(ob_ref[...].astype(_BF16), w_bb_ref[...])
    y = _sigmoid(proj_ref[:, OFF_ZA:OFF_ZA + D_MODEL]) * ya + _sigmoid(proj_ref[:, OFF_ZB:OFF_ZB + D_MODEL]) * yb
    y = _dot(y.astype(_BF16), w_out_ref[...])
    x1 = x + _rms(y, n_post_ref[...])
    e = _rms(_dot(p_ref[0, 0].astype(_BF16), w_ple_ref[...]), n_ple_ref[...])
    out_ref[0] = x1 + _sigmoid(_dot(x1.astype(_BF16), w_pg_ref[...])) * e


def _pad_heads(w, dk):
    lead = w.shape[:-1]
    w = w.reshape(lead + (N_HEADS, dk))
    w = jnp.pad(w, [(0, 0)] * len(lead) + [(0, 0), (0, D_HEAD - dk)])
    return w.reshape(lead + (N_HEADS * D_HEAD,))


def _const_spec(shape):
    nd = len(shape)
    return pl.BlockSpec(shape, lambda b, t: (0,) * nd, pipeline_mode=pl.Buffered(1))


def kernel(x, p, w_in, lb_logits, w_gla_up, b_gla, norm_pre, norm_post, head_norm_a, head_norm_b,
           w_branch_a, w_branch_b, w_out, w_ple, w_ple_gate, norm_ple):
    bsz, seq, _ = x.shape
    assert x.shape == (bsz, seq, D_MODEL) and seq % TL == 0 and w_in.shape[0] == 1 and p.shape[0] == 1

    wi = w_in[0]
    widths = (D_BR, D_BR, D_BR, D_BR, N_HEADS * DK_B, N_HEADS * DK_B, D_BR, D_BR, GLA_RANK, D_MODEL, D_MODEL)
    splits = [int(s) for s in np.cumsum(widths)[:-1]]
    qa, fa, ia, ga, qb, kb, vb, gb, ab, za, zb = jnp.split(wi, splits, axis=-1)
    ab = jnp.pad(ab, ((0, 0), (0, RANK_PAD - GLA_RANK)))
    w_cat = jnp.concatenate([za, zb, qa, fa, ia, ga, _pad_heads(qb, DK_B), _pad_heads(kb, DK_B), vb, gb, ab],
                            axis=-1).astype(_BF16)
    assert w_cat.shape == (D_MODEL, D_IN_PAD)
    w_up = jnp.pad(_pad_heads(w_gla_up[0], DK_B), ((0, RANK_PAD - GLA_RANK), (0, 0))).astype(_BF16)
    b_up = _pad_heads(b_gla[0], DK_B)[None, :]

    a_cat, lvl2 = _level_constants()
    amat = jnp.asarray(a_cat, dtype=_BF16)
    lvl2 = jnp.asarray(lvl2)

    row = lambda a: a[0][None, :]
    operands = (
        x, p, w_cat, lb_logits, w_up, b_up, row(norm_pre), row(norm_post), row(head_norm_a), row(head_norm_b),
        w_branch_a[0].astype(_BF16), w_branch_b[0].astype(_BF16), w_out[0].astype(_BF16),
        w_ple[0].astype(_BF16), w_ple_gate[0].astype(_BF16), row(norm_ple), amat, lvl2,
    )
    in_specs = [
        pl.BlockSpec((1, TL, D_MODEL), lambda b, t: (b, t, 0)),
        pl.BlockSpec((1, 1, TL, D_PLE), lambda b, t: (0, b, t, 0)),
    ] + [_const_spec(a.shape) for a in operands[2:]]

    return pl.pallas_call(
        _block_kernel,
        grid=(bsz, seq // TL),
        in_specs=in_specs,
        out_specs=pl.BlockSpec((1, TL, D_MODEL), lambda b, t: (b, t, 0)),
        out_shape=jax.ShapeDtypeStruct(x.shape, x.dtype),
        scratch_shapes=[
            pltpu.VMEM((TL, D_IN_PAD), _F32),
            pltpu.VMEM(((N_MXU_LEVELS + 1) * CH, D_BR), _F32),
            pltpu.VMEM(((N_MXU_LEVELS + 1) * CH, D_BR), _F32),
            pltpu.VMEM((2, CH, D_BR), _F32),
            pltpu.VMEM((2, CH, D_BR), _F32),
            pltpu.VMEM((N_OPERAND_ROWS, CH, D_BR), _BF16),
            pltpu.VMEM((N_OPERAND_ROWS, CH, D_BR), _BF16),
            pltpu.VMEM((N_HEADS // 2, CH, D_PAIR), _F32),
            pltpu.VMEM((N_HEADS // 2, CH, D_PAIR), _F32),
            pltpu.VMEM((N_HEADS, D_HEAD, D_HEAD), _F32),
            pltpu.VMEM((N_HEADS, D_HEAD, D_HEAD), _F32),
            pltpu.VMEM((TL, D_BR), _F32),
            pltpu.VMEM((TL, D_BR), _F32),
        ],
        compiler_params=pltpu.CompilerParams(
            dimension_semantics=("arbitrary", "arbitrary"),
            vmem_limit_bytes=V7X_VMEM_LIMIT_BYTES,
        ),
        name="hgrn2_gla_block",
    )(*operands)
```

```python
import numpy as np
import jax
import jax.numpy as jnp
from jax import lax
from jax.experimental import pallas as pl
from jax.experimental.pallas import tpu as pltpu

D_MODEL = 1024
D_PLE = 256
N_HEADS = 4
D_HEAD = 128
D_BR = N_HEADS * D_HEAD
D_PAIR = 2 * D_HEAD
DK_A = 128
DK_B = 64
GLA_RANK = 16
GLA_TAU = 16.0
NORM_EPS = 1e-6
LOG2_E = 1.4426950408889634

SUBLANES = 8
BF16_ROWS = 16
N_MXU_LEVELS = 3

CH = 128
N_LEVELS = 7
TL = 512
RANK_PAD = 128

ROW_QT = N_LEVELS
ROW_KH = N_LEVELS + 1
ROW_V = N_LEVELS + 2
N_OPERAND_ROWS = N_LEVELS + 3

OFF_ZA = 0
OFF_ZB = OFF_ZA + D_MODEL
OFF_QA = OFF_ZB + D_MODEL
OFF_FA = OFF_QA + D_BR
OFF_IA = OFF_FA + D_BR
OFF_GA = OFF_IA + D_BR
OFF_QB = OFF_GA + D_BR
OFF_KB = OFF_QB + D_BR
OFF_VB = OFF_KB + D_BR
OFF_GB = OFF_VB + D_BR
OFF_AB = OFF_GB + D_BR
D_IN_PAD = OFF_AB + RANK_PAD

V7X_VMEM_LIMIT_BYTES = 56 * 1024 * 1024

_F32 = jnp.float32
_BF16 = jnp.bfloat16


def _dot(a, b):
    return jnp.dot(a, b, preferred_element_type=_F32)


def _dot_nt(a, b):
    return lax.dot_general(a, b, (((1,), (1,)), ((), ())), preferred_element_type=_F32)


def _dot_tn(a, b):
    return lax.dot_general(a, b, (((0,), (0,)), ((), ())), preferred_element_type=_F32)


def _rms(x, gain):
    ms = jnp.mean(x * x, axis=-1, keepdims=True)
    return x * lax.rsqrt(ms + NORM_EPS) * gain


def _sigmoid(x):
    return 0.5 * jnp.tanh(0.5 * x) + 0.5


def _silu(x):
    hx = 0.5 * x
    return hx * jnp.tanh(hx) + hx


def _block_diag(a, b):
    zero = jnp.zeros_like(a)
    return jnp.concatenate([jnp.concatenate([a, zero], axis=1), jnp.concatenate([zero, b], axis=1)], axis=0)


def _level_constants():
    t = np.arange(CH)
    mats = []
    for l in range(N_MXU_LEVELS):
        s = 1 << l
        ref = (t & ~(2 * s - 1)) + s - 1
        odd = (t & s) != 0
        u = t[None, :]
        a = np.where(odd[:, None], (u > ref[:, None]) & (u <= t[:, None]),
                     (u > t[:, None]) & (u <= ref[:, None]))
        mats.append(a)
    mats.append(t[None, :] <= t[:, None])
    a_all = np.concatenate(mats, axis=0).astype(np.float32)
    a_cat = np.concatenate([a_all, a_all], axis=1)
    x = t[:, None] ^ t[None, :]
    lvl = np.where(x > 0, np.floor(np.log2(np.maximum(x, 1))).astype(np.int32), -1)
    lvl = np.where(t[:, None] > t[None, :], lvl, -1).astype(np.int32)
    return a_cat, np.concatenate([lvl, lvl], axis=1)


N_PREP_STAGES = 1 + N_HEADS
N_PAIR_STAGES = N_HEADS // 2


def _gla_chunk(inputs_fn, v_fn, gate_fn, gain, amat_ref, lvl2, e_ref, qk_ref, w_ref, sc_ref, st_ref, o_ref, rows):
    q, k, v, g = inputs_fn()
    qk_ref[0] = q
    qk_ref[1] = k
    w_ref[ROW_V] = v.astype(_BF16)
    g2 = g * LOG2_E
    g_hi = g2.astype(_BF16)
    g_lo = (g2 - g_hi.astype(_F32)).astype(_BF16)
    e_ref[...] = _dot(amat_ref[...], jnp.concatenate([g_hi, g_lo], axis=0))
    yield

    sub = lax.broadcasted_iota(jnp.int32, (CH, D_HEAD), 0)
    odd_rows = [(sub & (1 << l)) != 0 for l in range(N_MXU_LEVELS)]
    d_last = []
    for ct in range(N_HEADS):
        cs = slice(ct * D_HEAD, (ct + 1) * D_HEAD)
        qc = qk_ref[0, :, cs]
        kc = qk_ref[1, :, cs]
        big_g = e_ref[N_MXU_LEVELS * CH:(N_MXU_LEVELS + 1) * CH, cs]
        for l in range(N_LEVELS):
            s = 1 << l
            if l < N_MXU_LEVELS:
                w = jnp.where(odd_rows[l], qc, kc) * jnp.exp2(e_ref[l * CH:(l + 1) * CH, cs])
            else:
                parts = []
                for base in range(0, CH, 2 * s):
                    mid = base + s
                    g_ref_row = jnp.broadcast_to(big_g[mid - 1:mid, :], (s, D_HEAD))
                    parts.append(kc[base:mid] * jnp.exp2(g_ref_row - big_g[base:mid]))
                    parts.append(qc[mid:mid + s] * jnp.exp2(big_g[mid:mid + s] - g_ref_row))
                w = jnp.concatenate(parts, axis=0)
            w_ref[l, :, cs] = w.astype(_BF16)
        g_last = big_g[CH - 1:CH, :]
        w_ref[ROW_QT, :, cs] = (qc * jnp.exp2(big_g)).astype(_BF16)
        w_ref[ROW_KH, :, cs] = (kc * jnp.exp2(g_last - big_g)).astype(_BF16)
        d_last.append(jnp.exp2(g_last))
        yield

    for pr in range(N_HEADS // 2):
        h0 = slice(pr * D_PAIR, pr * D_PAIR + D_HEAD)
        h1 = slice(pr * D_PAIR + D_HEAD, (pr + 1) * D_PAIR)
        ps = slice(pr * D_PAIR, (pr + 1) * D_PAIR)

        def keys(idx, lanes, s):
            if s < BF16_ROWS:
                return w_ref[idx, :, lanes]
            zero = jnp.zeros((s, D_HEAD), _BF16)
            return jnp.concatenate([zero if b % 2 else w_ref[idx, b * s:(b + 1) * s, lanes]
                                    for b in range(CH // s)], axis=0)

        low = jnp.zeros((CH, D_PAIR), _F32)
        for l in range(N_LEVELS):
            s = 1 << l
            rhs = _block_diag(keys(l, h0, s), keys(l, h1, s))
            odd_blocks = [b for b in range(CH // s) if b % 2]
            if s >= BF16_ROWS:
                lhs = jnp.concatenate([w_ref[l, b * s:(b + 1) * s, ps] for b in odd_blocks], axis=0)
            else:
                lhs = w_ref[l, :, ps]
            r = _dot_nt(lhs, rhs)
            if s < SUBLANES:
                low = jnp.where(lvl2 == l, r, low)
                if 2 * s == SUBLANES:
                    sc_ref[pr] = low
                continue
            for i, b in enumerate(odd_blocks):
                src_row = i * s if s >= BF16_ROWS else b * s
                for head in range(2):
                    c0 = head * D_HEAD + (b - 1) * s
                    sc_ref[pr, b * s:(b + 1) * s, c0:c0 + s] = r[src_row:src_row + s, c0:c0 + s]
        sc = sc_ref[pr].astype(_BF16)

        st0 = st_ref[2 * pr]
        st1 = st_ref[2 * pr + 1]
        o = (_dot(sc, _block_diag(w_ref[ROW_V, :, h0], w_ref[ROW_V, :, h1]))
             + _dot_nt(w_ref[ROW_QT, :, ps], _block_diag(st0.astype(_BF16), st1.astype(_BF16))))
        st_ref[2 * pr] = st0 * d_last[2 * pr] + _dot_tn(w_ref[ROW_V, :, h0], w_ref[ROW_KH, :, h0])
        st_ref[2 * pr + 1] = st1 * d_last[2 * pr + 1] + _dot_tn(w_ref[ROW_V, :, h1], w_ref[ROW_KH, :, h1])
        for i, hs in enumerate((h0, h1)):
            self_score = jnp.sum(qk_ref[0, :, hs] * qk_ref[1, :, hs], axis=-1, keepdims=True)
            o_h = o[:, i * D_HEAD:(i + 1) * D_HEAD] + self_score * v_fn(hs)
            o_ref[rows, hs] = _rms(o_h, gain[:, hs]) * _silu(gate_fn(hs))
        yield


def _block_kernel(x_ref, p_ref, w_in_ref, lb_ref, w_up_ref, b_gla_ref, n_pre_ref, n_post_ref,
                  hn_a_ref, hn_b_ref, w_ba_ref, w_bb_ref, w_out_ref, w_ple_ref, w_pg_ref,
                  n_ple_ref, amat_ref, lvl_ref, out_ref,
                  proj_ref, ea_ref, eb_ref, qka_ref, qkb_ref, wa_ref, wb_ref, sca_ref, scb_ref, st_a_ref, st_b_ref, oa_ref, ob_ref):
    @pl.when(pl.program_id(1) == 0)
    def _():
        st_a_ref[...] = jnp.zeros_like(st_a_ref)
        st_b_ref[...] = jnp.zeros_like(st_b_ref)

    x = x_ref[0]
    proj_ref[...] = _dot(_rms(x, n_pre_ref[...]).astype(_BF16), w_in_ref[...])

    logits = lb_ref[...]
    ex = jnp.exp(logits - jnp.max(logits, axis=0, keepdims=True))
    lb = ex[0:1, :] / jnp.sum(ex, axis=0, keepdims=True)
    fg_half = 0.5 * (1.0 - lb)
    fg_mid = lb + fg_half
    lvl2 = lvl_ref[...]

    streams = []
    for c in range(TL // CH):
        rows = slice(c * CH, (c + 1) * CH)

        def col(off, width=D_BR, rows=rows):
            return proj_ref[rows, off:off + width]

        def inputs_a(col=col):
            forget = fg_mid + fg_half * jnp.tanh(0.5 * col(OFF_FA))
            return _silu(col(OFF_QA)) * (DK_A ** -0.5), 1.0 - forget, col(OFF_IA), jnp.log(forget)

        def inputs_b(col=col):
            z = _dot(col(OFF_AB, RANK_PAD).astype(_BF16), w_up_ref[...]) + b_gla_ref[...]
            log_alpha = -(jnp.maximum(-z, 0.0) + jnp.log(1.0 + jnp.exp(-jnp.abs(z)))) / GLA_TAU
            return col(OFF_QB) * (DK_B ** -0.5), col(OFF_KB), col(OFF_VB), log_alpha

        def lanes_of(off, col=col):
            return lambda hs: col(off + hs.start, hs.stop - hs.start)

        streams.append(_gla_chunk(inputs_a, lanes_of(OFF_IA), lanes_of(OFF_GA), hn_a_ref[...],
                                  amat_ref, lvl2, ea_ref, qka_ref, wa_ref, sca_ref, st_a_ref, oa_ref, rows))
        streams.append(_gla_chunk(inputs_b, lanes_of(OFF_VB), lanes_of(OFF_GB), hn_b_ref[...],
                                  amat_ref, lvl2, eb_ref, qkb_ref, wb_ref, scb_ref, st_b_ref, ob_ref, rows))

    def advance(stream, n):
        for _ in range(n):
            next(stream, None)

    share = [N_PREP_STAGES // N_PAIR_STAGES + (j < N_PREP_STAGES % N_PAIR_STAGES) for j in range(N_PAIR_STAGES)]
    advance(streams[0], N_PREP_STAGES)
    for i, cur in enumerate(streams):
        nxt = streams[i + 1] if i + 1 < len(streams) else None
        for j in range(N_PAIR_STAGES):
            if nxt is not None:
                advance(nxt, 1)
            advance(cur, 1)
            if nxt is not None:
                advance(nxt, share[j] - 1)

    ya = _dot(oa_ref[...].astype(_BF16), w_ba_ref[...])
    yb = _dot(ob_ref[...].astype(_BF16), w_bb_ref[...])
    y = _sigmoid(proj_ref[:, OFF_ZA:OFF_ZA + D_MODEL]) * ya + _sigmoid(proj_ref[:, OFF_ZB:OFF_ZB + D_MODEL]) * yb
    y = _dot(y.astype(_BF16), w_out_ref[...])
    x1 = x + _rms(y, n_post_ref[...])
    e = _rms(_dot(p_ref[0, 0].astype(_BF16), w_ple_ref[...]), n_ple_ref[...])
    out_ref[0] = x1 + _sigmoid(_dot(x1.astype(_BF16), w_pg_ref[...])) * e


def _pad_heads(w, dk):
    lead = w.shape[:-1]
    w = w.reshape(lead + (N_HEADS, dk))
    w = jnp.pad(w, [(0, 0)] * len(lead) + [(0, 0), (0, D_HEAD - dk)])
    return w.reshape(lead + (N_HEADS * D_HEAD,))


def _const_spec(shape):
    nd = len(shape)
    return pl.BlockSpec(shape, lambda b, t: (0,) * nd, pipeline_mode=pl.Buffered(1))


def kernel(x, p, w_in, lb_logits, w_gla_up, b_gla, norm_pre, norm_post, head_norm_a, head_norm_b,
           w_branch_a, w_branch_b, w_out, w_ple, w_ple_gate, norm_ple):
    bsz, seq, _ = x.shape
    assert x.shape == (bsz, seq, D_MODEL) and seq % TL == 0 and w_in.shape[0] == 1 and p.shape[0] == 1

    wi = w_in[0]
    widths = (D_BR, D_BR, D_BR, D_BR, N_HEADS * DK_B, N_HEADS * DK_B, D_BR, D_BR, GLA_RANK, D_MODEL, D_MODEL)
    splits = [int(s) for s in np.cumsum(widths)[:-1]]
    qa, fa, ia, ga, qb, kb, vb, gb, ab, za, zb = jnp.split(wi, splits, axis=-1)
    ab = jnp.pad(ab, ((0, 0), (0, RANK_PAD - GLA_RANK)))
    w_cat = jnp.concatenate([za, zb, qa, fa, ia, ga, _pad_heads(qb, DK_B), _pad_heads(kb, DK_B), vb, gb, ab],
                            axis=-1).astype(_BF16)
    assert w_cat.shape == (D_MODEL, D_IN_PAD)
    w_up = jnp.pad(_pad_heads(w_gla_up[0], DK_B), ((0, RANK_PAD - GLA_RANK), (0, 0))).astype(_BF16)
    b_up = _pad_heads(b_gla[0], DK_B)[None, :]

    a_cat, lvl2 = _level_constants()
    amat = jnp.asarray(a_cat, dtype=_BF16)
    lvl2 = jnp.asarray(lvl2)

    row = lambda a: a[0][None, :]
    operands = (
        x, p, w_cat, lb_logits, w_up, b_up, row(norm_pre), row(norm_post), row(head_norm_a), row(head_norm_b),
        w_branch_a[0].astype(_BF16), w_branch_b[0].astype(_BF16), w_out[0].astype(_BF16),
        w_ple[0].astype(_BF16), w_ple_gate[0].astype(_BF16), row(norm_ple), amat, lvl2,
    )
    in_specs = [
        pl.BlockSpec((1, TL, D_MODEL), lambda b, t: (b, t, 0)),
        pl.BlockSpec((1, 1, TL, D_PLE), lambda b, t: (0, b, t, 0)),
    ] + [_const_spec(a.shape) for a in operands[2:]]

    return pl.pallas_call(
        _block_kernel,
        grid=(bsz, seq // TL),
        in_specs=in_specs,
        out_specs=pl.BlockSpec((1, TL, D_MODEL), lambda b, t: (b, t, 0)),
        out_shape=jax.ShapeDtypeStruct(x.shape, x.dtype),
        scratch_shapes=[
            pltpu.VMEM((TL, D_IN_PAD), _F32),
            pltpu.VMEM(((N_MXU_LEVELS + 1) * CH, D_BR), _F32),
            pltpu.VMEM(((N_MXU_LEVELS + 1) * CH, D_BR), _F32),
            pltpu.VMEM((2, CH, D_BR), _F32),
            pltpu.VMEM((2, CH, D_BR), _F32),
            pltpu.VMEM((N_OPERAND_ROWS, CH, D_BR), _BF16),
            pltpu.VMEM((N_OPERAND_ROWS, CH, D_BR), _BF16),
            pltpu.VMEM((N_HEADS // 2, CH, D_PAIR), _F32),
            pltpu.VMEM((N_HEADS // 2, CH, D_PAIR), _F32),
            pltpu.VMEM((N_HEADS, D_HEAD, D_HEAD), _F32),
            pltpu.VMEM((N_HEADS, D_HEAD, D_HEAD), _F32),
            pltpu.VMEM((TL, D_BR), _F32),
            pltpu.VMEM((TL, D_BR), _F32),
        ],
        compiler_params=pltpu.CompilerParams(
            dimension_semantics=("arbitrary", "arbitrary"),
            vmem_limit_bytes=V7X_VMEM_LIMIT_BYTES,
        ),
        name="hgrn2_gla_block",
    )(*operands)
```

```python
import numpy as np
import jax
import jax.numpy as jnp
from jax import lax
from jax.experimental import pallas as pl
from jax.experimental.pallas import tpu as pltpu

D_MODEL = 1024
D_PLE = 256
N_HEADS = 4
D_HEAD = 128
D_BR = N_HEADS * D_HEAD
D_PAIR = 2 * D_HEAD
DK_A = 128
DK_B = 64
GLA_RANK = 16
GLA_TAU = 16.0
NORM_EPS = 1e-6
LOG2_E = 1.4426950408889634

SUBLANES = 8
BF16_ROWS = 16
N_MXU_LEVELS = 3

CH = 128
N_LEVELS = 7
TL = 512
RANK_PAD = 128
IN_WIDTHS = (D_BR, D_BR, D_BR, D_BR, N_HEADS * DK_B, N_HEADS * DK_B, D_BR, D_BR, GLA_RANK, D_MODEL, D_MODEL)

ROW_QT = N_LEVELS
ROW_KH = N_LEVELS + 1
ROW_V = N_LEVELS + 2
N_OPERAND_ROWS = N_LEVELS + 3

OFF_ZA = 0
OFF_ZB = OFF_ZA + D_MODEL
OFF_QA = OFF_ZB + D_MODEL
OFF_FA = OFF_QA + D_BR
OFF_IA = OFF_FA + D_BR
OFF_GA = OFF_IA + D_BR
OFF_QB = OFF_GA + D_BR
OFF_KB = OFF_QB + D_BR
OFF_VB = OFF_KB + D_BR
OFF_GB = OFF_VB + D_BR
OFF_AB = OFF_GB + D_BR
D_IN_PAD = OFF_AB + RANK_PAD

V7X_VMEM_LIMIT_BYTES = 56 * 1024 * 1024

_F32 = jnp.float32
_BF16 = jnp.bfloat16


def _dot(a, b):
    return jnp.dot(a, b, preferred_element_type=_F32)


def _dot_nt(a, b):
    return lax.dot_general(a, b, (((1,), (1,)), ((), ())), preferred_element_type=_F32)


def _dot_tn(a, b):
    return lax.dot_general(a, b, (((0,), (0,)), ((), ())), preferred_element_type=_F32)


def _rms(x, gain):
    ms = jnp.mean(x * x, axis=-1, keepdims=True)
    return x * lax.rsqrt(ms + NORM_EPS) * gain


def _sigmoid(x):
    return 0.5 * jnp.tanh(0.5 * x) + 0.5


def _silu(x):
    hx = 0.5 * x
    return hx * jnp.tanh(hx) + hx


def _block_diag(a, b):
    zero = jnp.zeros_like(a)
    return jnp.concatenate([jnp.concatenate([a, zero], axis=1), jnp.concatenate([zero, b], axis=1)], axis=0)


def _level_constants():
    t = np.arange(CH)
    mats = []
    for l in range(N_MXU_LEVELS):
        s = 1 << l
        ref = (t & ~(2 * s - 1)) + s - 1
        odd = (t & s) != 0
        u = t[None, :]
        a = np.where(odd[:, None], (u > ref[:, None]) & (u <= t[:, None]),
                     (u > t[:, None]) & (u <= ref[:, None]))
        mats.append(a)
    mats.append(t[None, :] <= t[:, None])
    a_all = np.concatenate(mats, axis=0).astype(np.float32)
    a_cat = np.concatenate([a_all, a_all], axis=1)
    x = t[:, None] ^ t[None, :]
    lvl = np.where(x > 0, np.floor(np.log2(np.maximum(x, 1))).astype(np.int32), -1)
    lvl = np.where(t[:, None] > t[None, :], lvl, -1).astype(np.int32)
    return a_cat, np.concatenate([lvl, lvl], axis=1)


N_PREP_STAGES = 1 + N_HEADS
N_PAIR_STAGES = N_HEADS // 2


def _gla_chunk(inputs_fn, v_fn, gate_fn, gain, amat_ref, lvl2, e_ref, qk_ref, w_ref, sc_ref, st_ref, o_ref, rows):
    q, k, v, g = inputs_fn()
    qk_ref[0] = q
    qk_ref[1] = k
    w_ref[ROW_V] = v.astype(_BF16)
    g2 = g * LOG2_E
    g_hi = g2.astype(_BF16)
    g_lo = (g2 - g_hi.astype(_F32)).astype(_BF16)
    e_ref[...] = _dot(amat_ref[...], jnp.concatenate([g_hi, g_lo], axis=0))
    yield

    sub = lax.broadcasted_iota(jnp.int32, (CH, D_HEAD), 0)
    odd_rows = [(sub & (1 << l)) != 0 for l in range(N_MXU_LEVELS)]
    d_last = []
    for ct in range(N_HEADS):
        cs = slice(ct * D_HEAD, (ct + 1) * D_HEAD)
        qc = qk_ref[0, :, cs]
        kc = qk_ref[1, :, cs]
        big_g = e_ref[N_MXU_LEVELS * CH:(N_MXU_LEVELS + 1) * CH, cs]
        for l in range(N_LEVELS):
            s = 1 << l
            if l < N_MXU_LEVELS:
                w = jnp.where(odd_rows[l], qc, kc) * jnp.exp2(e_ref[l * CH:(l + 1) * CH, cs])
            else:
                parts = []
                for base in range(0, CH, 2 * s):
                    mid = base + s
                    g_ref_row = jnp.broadcast_to(big_g[mid - 1:mid, :], (s, D_HEAD))
                    parts.append(kc[base:mid] * jnp.exp2(g_ref_row - big_g[base:mid]))
                    parts.append(qc[mid:mid + s] * jnp.exp2(big_g[mid:mid + s] - g_ref_row))
                w = jnp.concatenate(parts, axis=0)
            w_ref[l, :, cs] = w.astype(_BF16)
        g_last = big_g[CH - 1:CH, :]
        w_ref[ROW_QT, :, cs] = (qc * jnp.exp2(big_g)).astype(_BF16)
        w_ref[ROW_KH, :, cs] = (kc * jnp.exp2(g_last - big_g)).astype(_BF16)
        d_last.append(jnp.exp2(g_last))
        yield

    for pr in range(N_HEADS // 2):
        h0 = slice(pr * D_PAIR, pr * D_PAIR + D_HEAD)
        h1 = slice(pr * D_PAIR + D_HEAD, (pr + 1) * D_PAIR)
        ps = slice(pr * D_PAIR, (pr + 1) * D_PAIR)

        def keys(idx, lanes, s):
            if s < BF16_ROWS:
                return w_ref[idx, :, lanes]
            zero = jnp.zeros((s, D_HEAD), _BF16)
            return jnp.concatenate([zero if b % 2 else w_ref[idx, b * s:(b + 1) * s, lanes]
                                    for b in range(CH // s)], axis=0)

        low = jnp.zeros((CH, D_PAIR), _F32)
        for l in range(N_LEVELS):
            s = 1 << l
            rhs = _block_diag(keys(l, h0, s), keys(l, h1, s))
            odd_blocks = [b for b in range(CH // s) if b % 2]
            if s >= BF16_ROWS:
                lhs = jnp.concatenate([w_ref[l, b * s:(b + 1) * s, ps] for b in odd_blocks], axis=0)
            else:
                lhs = w_ref[l, :, ps]
            r = _dot_nt(lhs, rhs)
            if s < SUBLANES:
                low = jnp.where(lvl2 == l, r, low)
                if 2 * s == SUBLANES:
                    sc_ref[pr] = low
                continue
            for i, b in enumerate(odd_blocks):
                src_row = i * s if s >= BF16_ROWS else b * s
                for head in range(2):
                    c0 = head * D_HEAD + (b - 1) * s
                    sc_ref[pr, b * s:(b + 1) * s, c0:c0 + s] = r[src_row:src_row + s, c0:c0 + s]
        sc = sc_ref[pr].astype(_BF16)

        st0 = st_ref[2 * pr]
        st1 = st_ref[2 * pr + 1]
        o = (_dot(sc, _block_diag(w_ref[ROW_V, :, h0], w_ref[ROW_V, :, h1]))
             + _dot_nt(w_ref[ROW_QT, :, ps], _block_diag(st0.astype(_BF16), st1.astype(_BF16))))
        st_ref[2 * pr] = st0 * d_last[2 * pr] + _dot_tn(w_ref[ROW_V, :, h0], w_ref[ROW_KH, :, h0])
        st_ref[2 * pr + 1] = st1 * d_last[2 * pr + 1] + _dot_tn(w_ref[ROW_V, :, h1], w_ref[ROW_KH, :, h1])
        for i, hs in enumerate((h0, h1)):
            self_score = jnp.sum(qk_ref[0, :, hs] * qk_ref[1, :, hs], axis=-1, keepdims=True)
            o_h = o[:, i * D_HEAD:(i + 1) * D_HEAD] + self_score * v_fn(hs)
            o_ref[rows, hs] = _rms(o_h, gain[:, hs]) * _silu(gate_fn(hs))
        yield


def _block_kernel(x_ref, p_ref, w_in_ref, lb_ref, w_up_ref, b_gla_ref, n_pre_ref, n_post_ref,
                  hn_a_ref, hn_b_ref, w_ba_ref, w_bb_ref, w_out_ref, w_ple_ref, w_pg_ref,
                  n_ple_ref, amat_ref, lvl_ref, out_ref,
                  proj_ref, ea_ref, eb_ref, qka_ref, qkb_ref, wa_ref, wb_ref, sca_ref, scb_ref, st_a_ref, st_b_ref, oa_ref, ob_ref):
    @pl.when(pl.program_id(1) == 0)
    def _():
        st_a_ref[...] = jnp.zeros_like(st_a_ref)
        st_b_ref[...] = jnp.zeros_like(st_b_ref)

    x = x_ref[0]
    proj_ref[...] = _dot(_rms(x, n_pre_ref[...]).astype(_BF16), w_in_ref[...])

    logits = lb_ref[...]
    ex = jnp.exp(logits - jnp.max(logits, axis=0, keepdims=True))
    lb = ex[0:1, :] / jnp.sum(ex, axis=0, keepdims=True)
    fg_half = 0.5 * (1.0 - lb)
    fg_mid = lb + fg_half
    lvl2 = lvl_ref[...]

    streams = []
    for c in range(TL // CH):
        rows = slice(c * CH, (c + 1) * CH)

        def col(off, width=D_BR, rows=rows):
            return proj_ref[rows, off:off + width]

        def inputs_a(col=col):
            forget = fg_mid + fg_half * jnp.tanh(0.5 * col(OFF_FA))
            return _silu(col(OFF_QA)) * (DK_A ** -0.5), 1.0 - forget, col(OFF_IA), jnp.log(forget)

        def inputs_b(col=col):
            z = _dot(col(OFF_AB, RANK_PAD).astype(_BF16), w_up_ref[...]) + b_gla_ref[...]
            log_alpha = -(jnp.maximum(-z, 0.0) + jnp.log(1.0 + jnp.exp(-jnp.abs(z)))) / GLA_TAU
            return col(OFF_QB) * (DK_B ** -0.5), col(OFF_KB), col(OFF_VB), log_alpha

        def lanes_of(off, col=col):
            return lambda hs: col(off + hs.start, hs.stop - hs.start)

        streams.append(_gla_chunk(inputs_a, lanes_of(OFF_IA), lanes_of(OFF_GA), hn_a_ref[...],
                                  amat_ref, lvl2, ea_ref, qka_ref, wa_ref, sca_ref, st_a_ref, oa_ref, rows))
        streams.append(_gla_chunk(inputs_b, lanes_of(OFF_VB), lanes_of(OFF_GB), hn_b_ref[...],
                                  amat_ref, lvl2, eb_ref, qkb_ref, wb_ref, scb_ref, st_b_ref, ob_ref, rows))

    def advance(stream, n):
        for _ in range(n):
            next(stream, None)

    share = [N_PREP_STAGES // N_PAIR_STAGES + (j < N_PREP_STAGES % N_PAIR_STAGES) for j in range(N_PAIR_STAGES)]
    advance(streams[0], N_PREP_STAGES)
    for i, cur in enumerate(streams):
        nxt = streams[i + 1] if i + 1 < len(streams) else None
        for j in range(N_PAIR_STAGES):
            if nxt is not None:
                advance(nxt, 1)
            advance(cur, 1)
            if nxt is not None:
                advance(nxt, share[j] - 1)

    ya = _dot(oa_ref[...].astype(_BF16), w_ba_ref[...])
    yb = _dot(ob_ref[...].astype(_BF16), w_bb_ref[...])
    y = _sigmoid(proj_ref[:, OFF_ZA:OFF_ZA + D_MODEL]) * ya + _sigmoid(proj_ref[:, OFF_ZB:OFF_ZB + D_MODEL]) * yb
    y = _dot(y.astype(_BF16), w_out_ref[...])
    x1 = x + _rms(y, n_post_ref[...])
    e = _rms(_dot(p_ref[0, 0].astype(_BF16), w_ple_ref[...]), n_ple_ref[...])
    out_ref[0] = x1 + _sigmoid(_dot(x1.astype(_BF16), w_pg_ref[...])) * e


PREP_STEPS = 8


def _prep_kernel(w_in_ref, w_ba_ref, w_bb_ref, w_out_ref, w_ple_ref, w_pg_ref,
                 cat_ref, ba_ref, bb_ref, out_ref, ple_ref, pg_ref):
    rows = cat_ref.shape[0]

    def put(dst, src, width):
        cat_ref[:, dst:dst + width] = w_in_ref[0, :, src:src + width].astype(_BF16)

    src_off = np.cumsum((0,) + IN_WIDTHS)
    qa, fa, ia, ga, qb, kb, vb, gb, ab, za, zb = (int(o) for o in src_off[:-1])
    for dst, src in ((OFF_QA, qa), (OFF_FA, fa), (OFF_IA, ia), (OFF_GA, ga), (OFF_VB, vb), (OFF_GB, gb)):
        put(dst, src, D_BR)
    put(OFF_ZA, za, D_MODEL)
    put(OFF_ZB, zb, D_MODEL)
    pad = jnp.zeros((rows, D_HEAD - DK_B), _BF16)
    for dst, src in ((OFF_QB, qb), (OFF_KB, kb)):
        for h in range(N_HEADS):
            head = w_in_ref[0, :, src + h * DK_B:src + (h + 1) * DK_B].astype(_BF16)
            cat_ref[:, dst + h * D_HEAD:dst + (h + 1) * D_HEAD] = jnp.concatenate([head, pad], axis=1)
    rank = w_in_ref[0, :, ab:ab + GLA_RANK].astype(_BF16)
    cat_ref[:, OFF_AB:OFF_AB + RANK_PAD] = jnp.concatenate(
        [rank, jnp.zeros((rows, RANK_PAD - GLA_RANK), _BF16)], axis=1)

    for dst_ref, src_ref in ((ba_ref, w_ba_ref), (bb_ref, w_bb_ref), (out_ref, w_out_ref),
                             (ple_ref, w_ple_ref), (pg_ref, w_pg_ref)):
        dst_ref[...] = src_ref[0].astype(_BF16)


def _prep_weights(w_in, w_branch_a, w_branch_b, w_out, w_ple, w_ple_gate):
    srcs = (w_in, w_branch_a, w_branch_b, w_out, w_ple, w_ple_gate)
    out_cols = (D_IN_PAD,) + tuple(w.shape[2] for w in srcs[1:])
    for w in srcs:
        assert w.shape[0] == 1 and w.shape[1] % (PREP_STEPS * BF16_ROWS) == 0
    return pl.pallas_call(
        _prep_kernel,
        grid=(PREP_STEPS,),
        in_specs=[pl.BlockSpec((1, w.shape[1] // PREP_STEPS, w.shape[2]), lambda i: (0, i, 0)) for w in srcs],
        out_specs=[pl.BlockSpec((w.shape[1] // PREP_STEPS, c), lambda i: (i, 0)) for w, c in zip(srcs, out_cols)],
        out_shape=[jax.ShapeDtypeStruct((w.shape[1], c), _BF16) for w, c in zip(srcs, out_cols)],
        compiler_params=pltpu.CompilerParams(dimension_semantics=("arbitrary",)),
        name="cast_weights",
    )(*srcs)


def _pad_heads(w, dk):
    lead = w.shape[:-1]
    w = w.reshape(lead + (N_HEADS, dk))
    w = jnp.pad(w, [(0, 0)] * len(lead) + [(0, 0), (0, D_HEAD - dk)])
    return w.reshape(lead + (N_HEADS * D_HEAD,))


def _const_spec(shape):
    nd = len(shape)
    return pl.BlockSpec(shape, lambda b, t: (0,) * nd, pipeline_mode=pl.Buffered(1))


def kernel(x, p, w_in, lb_logits, w_gla_up, b_gla, norm_pre, norm_post, head_norm_a, head_norm_b,
           w_branch_a, w_branch_b, w_out, w_ple, w_ple_gate, norm_ple):
    bsz, seq, _ = x.shape
    assert x.shape == (bsz, seq, D_MODEL) and seq % TL == 0 and w_in.shape[0] == 1 and p.shape[0] == 1

    assert w_in.shape == (1, D_MODEL, sum(IN_WIDTHS))
    w_cat, w_ba, w_bb, w_o, w_pl, w_pg = _prep_weights(w_in, w_branch_a, w_branch_b, w_out, w_ple, w_ple_gate)
    w_up = jnp.pad(_pad_heads(w_gla_up[0], DK_B), ((0, RANK_PAD - GLA_RANK), (0, 0))).astype(_BF16)
    b_up = _pad_heads(b_gla[0], DK_B)[None, :]

    a_cat, lvl2 = _level_constants()
    amat = jnp.asarray(a_cat, dtype=_BF16)
    lvl2 = jnp.asarray(lvl2)

    row = lambda a: a[0][None, :]
    operands = (
        x, p, w_cat, lb_logits, w_up, b_up, row(norm_pre), row(norm_post), row(head_norm_a), row(head_norm_b),
        w_ba, w_bb, w_o, w_pl, w_pg, row(norm_ple), amat, lvl2,
    )
    in_specs = [
        pl.BlockSpec((1, TL, D_MODEL), lambda b, t: (b, t, 0)),
        pl.BlockSpec((1, 1, TL, D_PLE), lambda b, t: (0, b, t, 0)),
    ] + [_const_spec(a.shape) for a in operands[2:]]

    return pl.pallas_call(
        _block_kernel,
        grid=(bsz, seq // TL),
        in_specs=in_specs,
        out_specs=pl.BlockSpec((1, TL, D_MODEL), lambda b, t: (b, t, 0)),
        out_shape=jax.ShapeDtypeStruct(x.shape, x.dtype),
        scratch_shapes=[
            pltpu.VMEM((TL, D_IN_PAD), _F32),
            pltpu.VMEM(((N_MXU_LEVELS + 1) * CH, D_BR), _F32),
            pltpu.VMEM(((N_MXU_LEVELS + 1) * CH, D_BR), _F32),
            pltpu.VMEM((2, CH, D_BR), _F32),
            pltpu.VMEM((2, CH, D_BR), _F32),
            pltpu.VMEM((N_OPERAND_ROWS, CH, D_BR), _BF16),
            pltpu.VMEM((N_OPERAND_ROWS, CH, D_BR), _BF16),
            pltpu.VMEM((N_HEADS // 2, CH, D_PAIR), _F32),
            pltpu.VMEM((N_HEADS // 2, CH, D_PAIR), _F32),
            pltpu.VMEM((N_HEADS, D_HEAD, D_HEAD), _F32),
            pltpu.VMEM((N_HEADS, D_HEAD, D_HEAD), _F32),
            pltpu.VMEM((TL, D_BR), _F32),
            pltpu.VMEM((TL, D_BR), _F32),
        ],
        compiler_params=pltpu.CompilerParams(
            dimension_semantics=("arbitrary", "arbitrary"),
            vmem_limit_bytes=V7X_VMEM_LIMIT_BYTES,
        ),
        name="hgrn2_gla_block",
    )(*operands)
```

```python
import numpy as np
import jax
import jax.numpy as jnp
from jax import lax
from jax.experimental import pallas as pl
from jax.experimental.pallas import tpu as pltpu

D_MODEL = 1024
D_PLE = 256
N_HEADS = 4
D_HEAD = 128
D_BR = N_HEADS * D_HEAD
D_PAIR = 2 * D_HEAD
DK_A = 128
DK_B = 64
GLA_RANK = 16
GLA_TAU = 16.0
NORM_EPS = 1e-6
LOG2_E = 1.4426950408889634

SUBLANES = 8
BF16_ROWS = 16
N_MXU_LEVELS = 3

CH = 128
N_LEVELS = 7
TL = 512
RANK_PAD = 128
IN_WIDTHS = (D_BR, D_BR, D_BR, D_BR, N_HEADS * DK_B, N_HEADS * DK_B, D_BR, D_BR, GLA_RANK, D_MODEL, D_MODEL)

ROW_QT = N_LEVELS
ROW_KH = N_LEVELS + 1
ROW_V = N_LEVELS + 2
N_OPERAND_ROWS = N_LEVELS + 3

OFF_ZA = 0
OFF_ZB = OFF_ZA + D_MODEL
OFF_QA = OFF_ZB + D_MODEL
OFF_FA = OFF_QA + D_BR
OFF_IA = OFF_FA + D_BR
OFF_GA = OFF_IA + D_BR
OFF_QB = OFF_GA + D_BR
OFF_KB = OFF_QB + D_BR
OFF_VB = OFF_KB + D_BR
OFF_GB = OFF_VB + D_BR
OFF_AB = OFF_GB + D_BR
D_IN_PAD = OFF_AB + RANK_PAD

V7X_VMEM_LIMIT_BYTES = 56 * 1024 * 1024

_F32 = jnp.float32
_BF16 = jnp.bfloat16


def _dot(a, b):
    return jnp.dot(a, b, preferred_element_type=_F32)


def _dot_nt(a, b):
    return lax.dot_general(a, b, (((1,), (1,)), ((), ())), preferred_element_type=_F32)


def _dot_tn(a, b):
    return lax.dot_general(a, b, (((0,), (0,)), ((), ())), preferred_element_type=_F32)


def _rms(x, gain):
    ms = jnp.mean(x * x, axis=-1, keepdims=True)
    return x * lax.rsqrt(ms + NORM_EPS) * gain


def _sigmoid(x):
    return 0.5 * jnp.tanh(0.5 * x) + 0.5


def _silu(x):
    hx = 0.5 * x
    return hx * jnp.tanh(hx) + hx


def _block_diag(a, b):
    zero = jnp.zeros_like(a)
    return jnp.concatenate([jnp.concatenate([a, zero], axis=1), jnp.concatenate([zero, b], axis=1)], axis=0)


def _level_constants():
    t = np.arange(CH)
    mats = []
    for l in range(N_MXU_LEVELS):
        s = 1 << l
        ref = (t & ~(2 * s - 1)) + s - 1
        odd = (t & s) != 0
        u = t[None, :]
        a = np.where(odd[:, None], (u > ref[:, None]) & (u <= t[:, None]),
                     (u > t[:, None]) & (u <= ref[:, None]))
        mats.append(a)
    mats.append(t[None, :] <= t[:, None])
    a_all = np.concatenate(mats, axis=0).astype(np.float32)
    a_cat = np.concatenate([a_all, a_all], axis=1)
    x = t[:, None] ^ t[None, :]
    lvl = np.where(x > 0, np.floor(np.log2(np.maximum(x, 1))).astype(np.int32), -1)
    lvl = np.where(t[:, None] > t[None, :], lvl, -1).astype(np.int32)
    return a_cat, np.concatenate([lvl, lvl], axis=1)


N_PREP_STAGES = 1 + N_HEADS
N_PAIR_STAGES = N_HEADS // 2


def _gla_chunk(inputs_fn, v_fn, gate_fn, gain, amat_ref, lvl2, e_ref, qk_ref, w_ref, sc_ref, st_ref, o_ref, rows):
    q, k, v, g = inputs_fn()
    qk_ref[0] = q
    qk_ref[1] = k
    w_ref[ROW_V] = v.astype(_BF16)
    g2 = g * LOG2_E
    g_hi = g2.astype(_BF16)
    g_lo = (g2 - g_hi.astype(_F32)).astype(_BF16)
    e_ref[...] = _dot(amat_ref[...], jnp.concatenate([g_hi, g_lo], axis=0))
    yield

    sub = lax.broadcasted_iota(jnp.int32, (CH, D_HEAD), 0)
    odd_rows = [(sub & (1 << l)) != 0 for l in range(N_MXU_LEVELS)]
    d_last = []
    for ct in range(N_HEADS):
        cs = slice(ct * D_HEAD, (ct + 1) * D_HEAD)
        qc = qk_ref[0, :, cs]
        kc = qk_ref[1, :, cs]
        big_g = e_ref[N_MXU_LEVELS * CH:(N_MXU_LEVELS + 1) * CH, cs]
        for l in range(N_LEVELS):
            s = 1 << l
            if l < N_MXU_LEVELS:
                w = jnp.where(odd_rows[l], qc, kc) * jnp.exp2(e_ref[l * CH:(l + 1) * CH, cs])
            else:
                parts = []
                for base in range(0, CH, 2 * s):
                    mid = base + s
                    g_ref_row = jnp.broadcast_to(big_g[mid - 1:mid, :], (s, D_HEAD))
                    parts.append(kc[base:mid] * jnp.exp2(g_ref_row - big_g[base:mid]))
                    parts.append(qc[mid:mid + s] * jnp.exp2(big_g[mid:mid + s] - g_ref_row))
                w = jnp.concatenate(parts, axis=0)
            w_ref[l, :, cs] = w.astype(_BF16)
        g_last = big_g[CH - 1:CH, :]
        w_ref[ROW_QT, :, cs] = (qc * jnp.exp2(big_g)).astype(_BF16)
        w_ref[ROW_KH, :, cs] = (kc * jnp.exp2(g_last - big_g)).astype(_BF16)
        d_last.append(jnp.exp2(g_last))
        yield

    for pr in range(N_HEADS // 2):
        h0 = slice(pr * D_PAIR, pr * D_PAIR + D_HEAD)
        h1 = slice(pr * D_PAIR + D_HEAD, (pr + 1) * D_PAIR)
        ps = slice(pr * D_PAIR, (pr + 1) * D_PAIR)

        def keys(idx, lanes, s):
            if s < BF16_ROWS:
                return w_ref[idx, :, lanes]
            zero = jnp.zeros((s, D_HEAD), _BF16)
            return jnp.concatenate([zero if b % 2 else w_ref[idx, b * s:(b + 1) * s, lanes]
                                    for b in range(CH // s)], axis=0)

        low = jnp.zeros((CH, D_PAIR), _F32)
        for l in range(N_LEVELS):
            s = 1 << l
            rhs = _block_diag(keys(l, h0, s), keys(l, h1, s))
            odd_blocks = [b for b in range(CH // s) if b % 2]
            if s >= BF16_ROWS:
                lhs = jnp.concatenate([w_ref[l, b * s:(b + 1) * s, ps] for b in odd_blocks], axis=0)
            else:
                lhs = w_ref[l, :, ps]
            r = _dot_nt(lhs, rhs)
            if s < SUBLANES:
                low = jnp.where(lvl2 == l, r, low)
                if 2 * s == SUBLANES:
                    sc_ref[pr] = low
                continue
            for i, b in enumerate(odd_blocks):
                src_row = i * s if s >= BF16_ROWS else b * s
                for head in range(2):
                    c0 = head * D_HEAD + (b - 1) * s
                    sc_ref[pr, b * s:(b + 1) * s, c0:c0 + s] = r[src_row:src_row + s, c0:c0 + s]
        sc = sc_ref[pr].astype(_BF16)

        st0 = st_ref[2 * pr]
        st1 = st_ref[2 * pr + 1]
        o = (_dot(sc, _block_diag(w_ref[ROW_V, :, h0], w_ref[ROW_V, :, h1]))
             + _dot_nt(w_ref[ROW_QT, :, ps], _block_diag(st0.astype(_BF16), st1.astype(_BF16))))
        st_ref[2 * pr] = st0 * d_last[2 * pr] + _dot_tn(w_ref[ROW_V, :, h0], w_ref[ROW_KH, :, h0])
        st_ref[2 * pr + 1] = st1 * d_last[2 * pr + 1] + _dot_tn(w_ref[ROW_V, :, h1], w_ref[ROW_KH, :, h1])
        for i, hs in enumerate((h0, h1)):
            self_score = jnp.sum(qk_ref[0, :, hs] * qk_ref[1, :, hs], axis=-1, keepdims=True)
            o_h = o[:, i * D_HEAD:(i + 1) * D_HEAD] + self_score * v_fn(hs)
            o_ref[rows, hs] = _rms(o_h, gain[:, hs]) * _silu(gate_fn(hs))
        yield


def _block_kernel(x_ref, p_ref, w_in_ref, lb_ref, w_up_ref, b_gla_ref, n_pre_ref, n_post_ref,
                  hn_a_ref, hn_b_ref, w_ba_ref, w_bb_ref, w_out_ref, w_ple_ref, w_pg_ref,
                  n_ple_ref, amat_ref, lvl_ref, out_ref,
                  proj_ref, ea_ref, eb_ref, qka_ref, qkb_ref, wa_ref, wb_ref, sca_ref, scb_ref, st_a_ref, st_b_ref, oa_ref, ob_ref):
    @pl.when(pl.program_id(1) == 0)
    def _():
        st_a_ref[...] = jnp.zeros_like(st_a_ref)
        st_b_ref[...] = jnp.zeros_like(st_b_ref)

    x = x_ref[0]
    proj_ref[...] = _dot_nt(_rms(x, n_pre_ref[...]).astype(_BF16), w_in_ref[...])

    logits = lb_ref[...]
    ex = jnp.exp(logits - jnp.max(logits, axis=0, keepdims=True))
    lb = ex[0:1, :] / jnp.sum(ex, axis=0, keepdims=True)
    fg_half = 0.5 * (1.0 - lb)
    fg_mid = lb + fg_half
    lvl2 = lvl_ref[...]

    streams = []
    for c in range(TL // CH):
        rows = slice(c * CH, (c + 1) * CH)

        def col(off, width=D_BR, rows=rows):
            return proj_ref[rows, off:off + width]

        def inputs_a(col=col):
            forget = fg_mid + fg_half * jnp.tanh(0.5 * col(OFF_FA))
            return _silu(col(OFF_QA)) * (DK_A ** -0.5), 1.0 - forget, col(OFF_IA), jnp.log(forget)

        def inputs_b(col=col):
            z = _dot(col(OFF_AB, RANK_PAD).astype(_BF16), w_up_ref[...]) + b_gla_ref[...]
            log_alpha = -(jnp.maximum(-z, 0.0) + jnp.log(1.0 + jnp.exp(-jnp.abs(z)))) / GLA_TAU
            return col(OFF_QB) * (DK_B ** -0.5), col(OFF_KB), col(OFF_VB), log_alpha

        def lanes_of(off, col=col):
            return lambda hs: col(off + hs.start, hs.stop - hs.start)

        streams.append(_gla_chunk(inputs_a, lanes_of(OFF_IA), lanes_of(OFF_GA), hn_a_ref[...],
                                  amat_ref, lvl2, ea_ref, qka_ref, wa_ref, sca_ref, st_a_ref, oa_ref, rows))
        streams.append(_gla_chunk(inputs_b, lanes_of(OFF_VB), lanes_of(OFF_GB), hn_b_ref[...],
                                  amat_ref, lvl2, eb_ref, qkb_ref, wb_ref, scb_ref, st_b_ref, ob_ref, rows))

    def advance(stream, n):
        for _ in range(n):
            next(stream, None)

    share = [N_PREP_STAGES // N_PAIR_STAGES + (j < N_PREP_STAGES % N_PAIR_STAGES) for j in range(N_PAIR_STAGES)]
    advance(streams[0], N_PREP_STAGES)
    for i, cur in enumerate(streams):
        nxt = streams[i + 1] if i + 1 < len(streams) else None
        for j in range(N_PAIR_STAGES):
            if nxt is not None:
                advance(nxt, 1)
            advance(cur, 1)
            if nxt is not None:
                advance(nxt, share[j] - 1)

    ya = _dot(oa_ref[...].astype(_BF16), w_ba_ref[...])
    yb = _dot(ob_ref[...].astype(_BF16), w_bb_ref[...])
    y = _sigmoid(proj_ref[:, OFF_ZA:OFF_ZA + D_MODEL]) * ya + _sigmoid(proj_ref[:, OFF_ZB:OFF_ZB + D_MODEL]) * yb
    y = _dot(y.astype(_BF16), w_out_ref[...])
    x1 = x + _rms(y, n_post_ref[...])
    e = _rms(_dot(p_ref[0, 0].astype(_BF16), w_ple_ref[...]), n_ple_ref[...])
    out_ref[0] = x1 + _sigmoid(_dot(x1.astype(_BF16), w_pg_ref[...])) * e


PREP_STEPS = 8


def _prep_kernel(w_in_t_ref, w_ba_ref, w_bb_ref, w_out_ref, w_ple_ref, w_pg_ref,
                 cat_ref, ba_ref, bb_ref, out_ref, ple_ref, pg_ref):
    cols = cat_ref.shape[1]

    def put(dst, src, n_rows):
        cat_ref[dst:dst + n_rows, :] = w_in_t_ref[src:src + n_rows, :].astype(_BF16)

    src_off = np.cumsum((0,) + IN_WIDTHS)
    qa, fa, ia, ga, qb, kb, vb, gb, ab, za, zb = (int(o) for o in src_off[:-1])
    for dst, src in ((OFF_QA, qa), (OFF_FA, fa), (OFF_IA, ia), (OFF_GA, ga), (OFF_VB, vb), (OFF_GB, gb)):
        put(dst, src, D_BR)
    put(OFF_ZA, za, D_MODEL)
    put(OFF_ZB, zb, D_MODEL)
    for dst, src in ((OFF_QB, qb), (OFF_KB, kb)):
        for h in range(N_HEADS):
            put(dst + h * D_HEAD, src + h * DK_B, DK_B)
            cat_ref[dst + h * D_HEAD + DK_B:dst + (h + 1) * D_HEAD, :] = jnp.zeros((D_HEAD - DK_B, cols), _BF16)
    put(OFF_AB, ab, GLA_RANK)
    cat_ref[OFF_AB + GLA_RANK:OFF_AB + RANK_PAD, :] = jnp.zeros((RANK_PAD - GLA_RANK, cols), _BF16)

    for dst_ref, src_ref in ((ba_ref, w_ba_ref), (bb_ref, w_bb_ref), (out_ref, w_out_ref),
                             (ple_ref, w_ple_ref), (pg_ref, w_pg_ref)):
        dst_ref[...] = src_ref[0].astype(_BF16)


def _prep_weights(w_in_t, w_branch_a, w_branch_b, w_out, w_ple, w_ple_gate):
    rest = (w_branch_a, w_branch_b, w_out, w_ple, w_ple_gate)
    for w in rest:
        assert w.shape[0] == 1 and w.shape[1] % (PREP_STEPS * BF16_ROWS) == 0
    lane_block = D_MODEL // PREP_STEPS
    return pl.pallas_call(
        _prep_kernel,
        grid=(PREP_STEPS,),
        in_specs=[pl.BlockSpec((w_in_t.shape[0], lane_block), lambda i: (0, i))]
        + [pl.BlockSpec((1, w.shape[1] // PREP_STEPS, w.shape[2]), lambda i: (0, i, 0)) for w in rest],
        out_specs=[pl.BlockSpec((D_IN_PAD, lane_block), lambda i: (0, i))]
        + [pl.BlockSpec((w.shape[1] // PREP_STEPS, w.shape[2]), lambda i: (i, 0)) for w in rest],
        out_shape=[jax.ShapeDtypeStruct((D_IN_PAD, D_MODEL), _BF16)]
        + [jax.ShapeDtypeStruct(w.shape[1:], _BF16) for w in rest],
        compiler_params=pltpu.CompilerParams(dimension_semantics=("arbitrary",),
                                             vmem_limit_bytes=V7X_VMEM_LIMIT_BYTES),
        name="cast_weights",
    )(w_in_t, *rest)


def _pad_heads(w, dk):
    lead = w.shape[:-1]
    w = w.reshape(lead + (N_HEADS, dk))
    w = jnp.pad(w, [(0, 0)] * len(lead) + [(0, 0), (0, D_HEAD - dk)])
    return w.reshape(lead + (N_HEADS * D_HEAD,))


def _const_spec(shape):
    nd = len(shape)
    return pl.BlockSpec(shape, lambda b, t: (0,) * nd, pipeline_mode=pl.Buffered(1))


def kernel(x, p, w_in, lb_logits, w_gla_up, b_gla, norm_pre, norm_post, head_norm_a, head_norm_b,
           w_branch_a, w_branch_b, w_out, w_ple, w_ple_gate, norm_ple):
    bsz, seq, _ = x.shape
    assert x.shape == (bsz, seq, D_MODEL) and seq % TL == 0 and w_in.shape[0] == 1 and p.shape[0] == 1

    assert w_in.shape == (1, D_MODEL, sum(IN_WIDTHS))
    w_cat, w_ba, w_bb, w_o, w_pl, w_pg = _prep_weights(w_in[0].T, w_branch_a, w_branch_b, w_out, w_ple, w_ple_gate)
    w_up = jnp.pad(_pad_heads(w_gla_up[0], DK_B), ((0, RANK_PAD - GLA_RANK), (0, 0))).astype(_BF16)
    b_up = _pad_heads(b_gla[0], DK_B)[None, :]

    a_cat, lvl2 = _level_constants()
    amat = jnp.asarray(a_cat, dtype=_BF16)
    lvl2 = jnp.asarray(lvl2)

    row = lambda a: a[0][None, :]
    operands = (
        x, p, w_cat, lb_logits, w_up, b_up, row(norm_pre), row(norm_post), row(head_norm_a), row(head_norm_b),
        w_ba, w_bb, w_o, w_pl, w_pg, row(norm_ple), amat, lvl2,
    )
    in_specs = [
        pl.BlockSpec((1, TL, D_MODEL), lambda b, t: (b, t, 0)),
        pl.BlockSpec((1, 1, TL, D_PLE), lambda b, t: (0, b, t, 0)),
    ] + [_const_spec(a.shape) for a in operands[2:]]

    return pl.pallas_call(
        _block_kernel,
        grid=(bsz, seq // TL),
        in_specs=in_specs,
        out_specs=pl.BlockSpec((1, TL, D_MODEL), lambda b, t: (b, t, 0)),
        out_shape=jax.ShapeDtypeStruct(x.shape, x.dtype),
        scratch_shapes=[
            pltpu.VMEM((TL, D_IN_PAD), _F32),
            pltpu.VMEM(((N_MXU_LEVELS + 1) * CH, D_BR), _F32),
            pltpu.VMEM(((N_MXU_LEVELS + 1) * CH, D_BR), _F32),
            pltpu.VMEM((2, CH, D_BR), _F32),
            pltpu.VMEM((2, CH, D_BR), _F32),
            pltpu.VMEM((N_OPERAND_ROWS, CH, D_BR), _BF16),
            pltpu.VMEM((N_OPERAND_ROWS, CH, D_BR), _BF16),
            pltpu.VMEM((N_HEADS // 2, CH, D_PAIR), _F32),
            pltpu.VMEM((N_HEADS // 2, CH, D_PAIR), _F32),
            pltpu.VMEM((N_HEADS, D_HEAD, D_HEAD), _F32),
            pltpu.VMEM((N_HEADS, D_HEAD, D_HEAD), _F32),
            pltpu.VMEM((TL, D_BR), _F32),
            pltpu.VMEM((TL, D_BR), _F32),
        ],
        compiler_params=pltpu.CompilerParams(
            dimension_semantics=("arbitrary", "arbitrary"),
            vmem_limit_bytes=V7X_VMEM_LIMIT_BYTES,
        ),
        name="hgrn2_gla_block",
    )(*operands)
```

```python
import numpy as np
import jax
import jax.numpy as jnp
from jax import lax
from jax.experimental import pallas as pl
from jax.experimental.pallas import tpu as pltpu

D_MODEL = 1024
D_PLE = 256
N_HEADS = 4
D_HEAD = 128
D_BR = N_HEADS * D_HEAD
D_PAIR = 2 * D_HEAD
DK_A = 128
DK_B = 64
GLA_RANK = 16
GLA_TAU = 16.0
NORM_EPS = 1e-6
LOG2_E = 1.4426950408889634

SUBLANES = 8
BF16_ROWS = 16
N_MXU_LEVELS = SUBLANES.bit_length() - 1

CH = 128
N_LEVELS = CH.bit_length() - 1
TL = 512
RANK_PAD = 128
IN_WIDTHS = (D_BR, D_BR, D_BR, D_BR, N_HEADS * DK_B, N_HEADS * DK_B, D_BR, D_BR, GLA_RANK, D_MODEL, D_MODEL)

ROW_QT = N_LEVELS
ROW_KH = N_LEVELS + 1
ROW_V = N_LEVELS + 2
N_OPERAND_ROWS = N_LEVELS + 3

OFF_ZA = 0
OFF_ZB = OFF_ZA + D_MODEL
OFF_QA = OFF_ZB + D_MODEL
OFF_FA = OFF_QA + D_BR
OFF_IA = OFF_FA + D_BR
OFF_GA = OFF_IA + D_BR
D_QK_B = N_HEADS * DK_B
OFF_QB = OFF_GA + D_BR
OFF_KB = OFF_QB + D_QK_B
OFF_VB = OFF_KB + D_QK_B
OFF_GB = OFF_VB + D_BR
OFF_AB = OFF_GB + D_BR
D_IN_PAD = OFF_AB + RANK_PAD

V7X_VMEM_LIMIT_BYTES = 56 * 1024 * 1024

_F32 = jnp.float32
_BF16 = jnp.bfloat16


def _dot(a, b):
    return jnp.dot(a, b, preferred_element_type=_F32)


def _dot_nt(a, b):
    return lax.dot_general(a, b, (((1,), (1,)), ((), ())), preferred_element_type=_F32)


def _dot_tn(a, b):
    return lax.dot_general(a, b, (((0,), (0,)), ((), ())), preferred_element_type=_F32)


def _rms(x, gain):
    ms = jnp.mean(x * x, axis=-1, keepdims=True)
    return x * lax.rsqrt(ms + NORM_EPS) * gain


def _sigmoid(x):
    return 0.5 * jnp.tanh(0.5 * x) + 0.5


def _silu(x):
    hx = 0.5 * x
    return hx * jnp.tanh(hx) + hx


def _block_diag(a, b):
    zero = jnp.zeros_like(a)
    return jnp.concatenate([jnp.concatenate([a, zero], axis=1), jnp.concatenate([zero, b], axis=1)], axis=0)


def _spread_heads(x):
    lane = lax.broadcasted_iota(jnp.int32, (x.shape[0], D_HEAD), 1)
    per_tile = D_HEAD // DK_B
    tiles = []
    for h in range(N_HEADS):
        t = x[:, (h // per_tile) * D_HEAD:(h // per_tile + 1) * D_HEAD]
        shift = (h % per_tile) * DK_B
        if shift:
            t = pltpu.roll(t, D_HEAD - shift, axis=1)
        tiles.append(jnp.where(lane < DK_B, t, 0.0))
    return jnp.concatenate(tiles, axis=1)


def _level_constants():
    t = np.arange(CH)
    mats = []
    for l in range(N_MXU_LEVELS):
        s = 1 << l
        ref = (t & ~(2 * s - 1)) + s - 1
        odd = (t & s) != 0
        u = t[None, :]
        a = np.where(odd[:, None], (u > ref[:, None]) & (u <= t[:, None]),
                     (u > t[:, None]) & (u <= ref[:, None]))
        mats.append(a)
    mats.append(t[None, :] <= t[:, None])
    a_all = np.concatenate(mats, axis=0).astype(np.float32)
    a_cat = np.concatenate([a_all, a_all], axis=1)
    x = t[:, None] ^ t[None, :]
    lvl = np.where(x > 0, np.floor(np.log2(np.maximum(x, 1))).astype(np.int32), -1)
    lvl = np.where(t[:, None] > t[None, :], lvl, -1).astype(np.int32)
    return a_cat, np.concatenate([lvl, lvl], axis=1)


N_PREP_STAGES = 1 + N_HEADS
N_PAIR_STAGES = N_HEADS // 2


def _gla_chunk(inputs_fn, v_fn, gate_fn, gain, amat_ref, lvl2, e_ref, qk_ref, w_ref, sc_ref, st_ref, o_ref, rows):
    q, k, v, g = inputs_fn()
    qk_ref[0] = q
    qk_ref[1] = k
    w_ref[ROW_V] = v.astype(_BF16)
    g2 = g * LOG2_E
    g_hi = g2.astype(_BF16)
    g_lo = (g2 - g_hi.astype(_F32)).astype(_BF16)
    e_ref[...] = _dot(amat_ref[...], jnp.concatenate([g_hi, g_lo], axis=0))
    yield

    sub = lax.broadcasted_iota(jnp.int32, (CH, D_HEAD), 0)
    odd_rows = [(sub & (1 << l)) != 0 for l in range(N_MXU_LEVELS)]
    d_last = []
    for ct in range(N_HEADS):
        cs = slice(ct * D_HEAD, (ct + 1) * D_HEAD)
        qc = qk_ref[0, :, cs]
        kc = qk_ref[1, :, cs]
        big_g = e_ref[N_MXU_LEVELS * CH:(N_MXU_LEVELS + 1) * CH, cs]
        for l in range(N_LEVELS):
            s = 1 << l
            if l < N_MXU_LEVELS:
                w = jnp.where(odd_rows[l], qc, kc) * jnp.exp2(e_ref[l * CH:(l + 1) * CH, cs])
            else:
                parts = []
                for base in range(0, CH, 2 * s):
                    mid = base + s
                    g_ref_row = jnp.broadcast_to(big_g[mid - 1:mid, :], (s, D_HEAD))
                    parts.append(kc[base:mid] * jnp.exp2(g_ref_row - big_g[base:mid]))
                    parts.append(qc[mid:mid + s] * jnp.exp2(big_g[mid:mid + s] - g_ref_row))
                w = jnp.concatenate(parts, axis=0)
            w_ref[l, :, cs] = w.astype(_BF16)
        g_last = big_g[CH - 1:CH, :]
        w_ref[ROW_QT, :, cs] = (qc * jnp.exp2(big_g)).astype(_BF16)
        w_ref[ROW_KH, :, cs] = (kc * jnp.exp2(g_last - big_g)).astype(_BF16)
        d_last.append(jnp.exp2(g_last))
        yield

    for pr in range(N_HEADS // 2):
        h0 = slice(pr * D_PAIR, pr * D_PAIR + D_HEAD)
        h1 = slice(pr * D_PAIR + D_HEAD, (pr + 1) * D_PAIR)
        ps = slice(pr * D_PAIR, (pr + 1) * D_PAIR)

        def keys(idx, lanes, s):
            if s < BF16_ROWS:
                return w_ref[idx, :, lanes]
            zero = jnp.zeros((s, D_HEAD), _BF16)
            return jnp.concatenate([zero if b % 2 else w_ref[idx, b * s:(b + 1) * s, lanes]
                                    for b in range(CH // s)], axis=0)

        low = jnp.zeros((CH, D_PAIR), _F32)
        for l in range(N_LEVELS):
            s = 1 << l
            rhs = _block_diag(keys(l, h0, s), keys(l, h1, s))
            odd_blocks = [b for b in range(CH // s) if b % 2]
            if s >= BF16_ROWS:
                lhs = jnp.concatenate([w_ref[l, b * s:(b + 1) * s, ps] for b in odd_blocks], axis=0)
            else:
                lhs = w_ref[l, :, ps]
            r = _dot_nt(lhs, rhs)
            if s < SUBLANES:
                low = jnp.where(lvl2 == l, r, low)
                if 2 * s == SUBLANES:
                    sc_ref[pr] = low
                continue
            for i, b in enumerate(odd_blocks):
                src_row = i * s if s >= BF16_ROWS else b * s
                for head in range(2):
                    c0 = head * D_HEAD + (b - 1) * s
                    sc_ref[pr, b * s:(b + 1) * s, c0:c0 + s] = r[src_row:src_row + s, c0:c0 + s]
        sc = sc_ref[pr].astype(_BF16)

        st0 = st_ref[2 * pr]
        st1 = st_ref[2 * pr + 1]
        o = (_dot(sc, _block_diag(w_ref[ROW_V, :, h0], w_ref[ROW_V, :, h1]))
             + _dot_nt(w_ref[ROW_QT, :, ps], _block_diag(st0.astype(_BF16), st1.astype(_BF16))))
        st_ref[2 * pr] = st0 * d_last[2 * pr] + _dot_tn(w_ref[ROW_V, :, h0], w_ref[ROW_KH, :, h0])
        st_ref[2 * pr + 1] = st1 * d_last[2 * pr + 1] + _dot_tn(w_ref[ROW_V, :, h1], w_ref[ROW_KH, :, h1])
        for i, hs in enumerate((h0, h1)):
            self_score = jnp.sum(qk_ref[0, :, hs] * qk_ref[1, :, hs], axis=-1, keepdims=True)
            o_h = o[:, i * D_HEAD:(i + 1) * D_HEAD] + self_score * v_fn(hs)
            o_ref[rows, hs] = _rms(o_h, gain[:, hs]) * _silu(gate_fn(hs))
        yield


def _block_kernel(x_ref, p_ref, w_in_ref, lb_ref, w_up_ref, b_gla_ref, n_pre_ref, n_post_ref,
                  hn_a_ref, hn_b_ref, w_ba_ref, w_bb_ref, w_out_ref, w_ple_ref, w_pg_ref,
                  n_ple_ref, amat_ref, lvl_ref, out_ref,
                  proj_ref, ea_ref, eb_ref, qka_ref, qkb_ref, wa_ref, wb_ref, sca_ref, scb_ref, st_a_ref, st_b_ref, oa_ref, ob_ref):
    @pl.when(pl.program_id(1) == 0)
    def _():
        st_a_ref[...] = jnp.zeros_like(st_a_ref)
        st_b_ref[...] = jnp.zeros_like(st_b_ref)

    x = x_ref[0]
    proj_ref[...] = _dot_nt(_rms(x, n_pre_ref[...]).astype(_BF16), w_in_ref[...])

    logits = lb_ref[...]
    ex = jnp.exp(logits - jnp.max(logits, axis=0, keepdims=True))
    lb = ex[0:1, :] / jnp.sum(ex, axis=0, keepdims=True)
    fg_half = 0.5 * (1.0 - lb)
    fg_mid = lb + fg_half
    lvl2 = lvl_ref[...]

    streams = []
    for c in range(TL // CH):
        rows = slice(c * CH, (c + 1) * CH)

        def col(off, width=D_BR, rows=rows):
            return proj_ref[rows, off:off + width]

        def inputs_a(col=col):
            forget = fg_mid + fg_half * jnp.tanh(0.5 * col(OFF_FA))
            return _silu(col(OFF_QA)) * (DK_A ** -0.5), 1.0 - forget, col(OFF_IA), jnp.log(forget)

        def inputs_b(col=col):
            z = _dot(col(OFF_AB, RANK_PAD).astype(_BF16), w_up_ref[...]) + b_gla_ref[...]
            log_alpha = -(jnp.maximum(-z, 0.0) + jnp.log(1.0 + jnp.exp(-jnp.abs(z)))) / GLA_TAU
            q = _spread_heads(col(OFF_QB, D_QK_B)) * (DK_B ** -0.5)
            return q, _spread_heads(col(OFF_KB, D_QK_B)), col(OFF_VB), log_alpha

        def lanes_of(off, col=col):
            return lambda hs: col(off + hs.start, hs.stop - hs.start)

        streams.append(_gla_chunk(inputs_a, lanes_of(OFF_IA), lanes_of(OFF_GA), hn_a_ref[...],
                                  amat_ref, lvl2, ea_ref, qka_ref, wa_ref, sca_ref, st_a_ref, oa_ref, rows))
        streams.append(_gla_chunk(inputs_b, lanes_of(OFF_VB), lanes_of(OFF_GB), hn_b_ref[...],
                                  amat_ref, lvl2, eb_ref, qkb_ref, wb_ref, scb_ref, st_b_ref, ob_ref, rows))

    def advance(stream, n):
        for _ in range(n):
            next(stream, None)

    share = [N_PREP_STAGES // N_PAIR_STAGES + (j < N_PREP_STAGES % N_PAIR_STAGES) for j in range(N_PAIR_STAGES)]
    advance(streams[0], N_PREP_STAGES)
    for i, cur in enumerate(streams):
        nxt = streams[i + 1] if i + 1 < len(streams) else None
        for j in range(N_PAIR_STAGES):
            if nxt is not None:
                advance(nxt, 1)
            advance(cur, 1)
            if nxt is not None:
                advance(nxt, share[j] - 1)

    ya = _dot(oa_ref[...].astype(_BF16), w_ba_ref[...])
    yb = _dot(ob_ref[...].astype(_BF16), w_bb_ref[...])
    y = _sigmoid(proj_ref[:, OFF_ZA:OFF_ZA + D_MODEL]) * ya + _sigmoid(proj_ref[:, OFF_ZB:OFF_ZB + D_MODEL]) * yb
    y = _dot(y.astype(_BF16), w_out_ref[...])
    x1 = x + _rms(y, n_post_ref[...])
    e = _rms(_dot(p_ref[0, 0].astype(_BF16), w_ple_ref[...]), n_ple_ref[...])
    out_ref[0] = x1 + _sigmoid(_dot(x1.astype(_BF16), w_pg_ref[...])) * e


PREP_STEPS = 8


def _prep_kernel(w_in_t_ref, w_ba_ref, w_bb_ref, w_out_ref, w_ple_ref, w_pg_ref,
                 cat_ref, ba_ref, bb_ref, out_ref, ple_ref, pg_ref):
    cols = cat_ref.shape[1]

    def put(dst, src, n_rows):
        cat_ref[dst:dst + n_rows, :] = w_in_t_ref[src:src + n_rows, :].astype(_BF16)

    src_off = np.cumsum((0,) + IN_WIDTHS)
    qa, fa, ia, ga, qb, kb, vb, gb, ab, za, zb = (int(o) for o in src_off[:-1])
    for dst, src in ((OFF_QA, qa), (OFF_FA, fa), (OFF_IA, ia), (OFF_GA, ga), (OFF_VB, vb), (OFF_GB, gb)):
        put(dst, src, D_BR)
    put(OFF_ZA, za, D_MODEL)
    put(OFF_ZB, zb, D_MODEL)
    put(OFF_QB, qb, D_QK_B)
    put(OFF_KB, kb, D_QK_B)
    put(OFF_AB, ab, GLA_RANK)
    cat_ref[OFF_AB + GLA_RANK:OFF_AB + RANK_PAD, :] = jnp.zeros((RANK_PAD - GLA_RANK, cols), _BF16)

    for dst_ref, src_ref in ((ba_ref, w_ba_ref), (bb_ref, w_bb_ref), (out_ref, w_out_ref),
                             (ple_ref, w_ple_ref), (pg_ref, w_pg_ref)):
        dst_ref[...] = src_ref[0].astype(_BF16)


def _prep_weights(w_in_t, w_branch_a, w_branch_b, w_out, w_ple, w_ple_gate):
    rest = (w_branch_a, w_branch_b, w_out, w_ple, w_ple_gate)
    for w in rest:
        assert w.shape[0] == 1 and w.shape[1] % (PREP_STEPS * BF16_ROWS) == 0
    lane_block = D_MODEL // PREP_STEPS
    return pl.pallas_call(
        _prep_kernel,
        grid=(PREP_STEPS,),
        in_specs=[pl.BlockSpec((w_in_t.shape[0], lane_block), lambda i: (0, i))]
        + [pl.BlockSpec((1, w.shape[1] // PREP_STEPS, w.shape[2]), lambda i: (0, i, 0)) for w in rest],
        out_specs=[pl.BlockSpec((D_IN_PAD, lane_block), lambda i: (0, i))]
        + [pl.BlockSpec((w.shape[1] // PREP_STEPS, w.shape[2]), lambda i: (i, 0)) for w in rest],
        out_shape=[jax.ShapeDtypeStruct((D_IN_PAD, D_MODEL), _BF16)]
        + [jax.ShapeDtypeStruct(w.shape[1:], _BF16) for w in rest],
        compiler_params=pltpu.CompilerParams(dimension_semantics=("arbitrary",),
                                             vmem_limit_bytes=V7X_VMEM_LIMIT_BYTES),
        name="cast_weights",
    )(w_in_t, *rest)


def _pad_heads(w, dk):
    lead = w.shape[:-1]
    w = w.reshape(lead + (N_HEADS, dk))
    w = jnp.pad(w, [(0, 0)] * len(lead) + [(0, 0), (0, D_HEAD - dk)])
    return w.reshape(lead + (N_HEADS * D_HEAD,))


def _const_spec(shape):
    nd = len(shape)
    return pl.BlockSpec(shape, lambda b, t: (0,) * nd, pipeline_mode=pl.Buffered(1))


def kernel(x, p, w_in, lb_logits, w_gla_up, b_gla, norm_pre, norm_post, head_norm_a, head_norm_b,
           w_branch_a, w_branch_b, w_out, w_ple, w_ple_gate, norm_ple):
    bsz, seq, _ = x.shape
    assert x.shape == (bsz, seq, D_MODEL) and seq % TL == 0 and w_in.shape[0] == 1 and p.shape[0] == 1

    assert w_in.shape == (1, D_MODEL, sum(IN_WIDTHS))
    w_cat, w_ba, w_bb, w_o, w_pl, w_pg = _prep_weights(w_in[0].T, w_branch_a, w_branch_b, w_out, w_ple, w_ple_gate)
    w_up = jnp.pad(_pad_heads(w_gla_up[0], DK_B), ((0, RANK_PAD - GLA_RANK), (0, 0))).astype(_BF16)
    b_up = _pad_heads(b_gla[0], DK_B)[None, :]

    a_cat, lvl2 = _level_constants()
    amat = jnp.asarray(a_cat, dtype=_BF16)
    lvl2 = jnp.asarray(lvl2)

    row = lambda a: a[0][None, :]
    operands = (
        x, p, w_cat, lb_logits, w_up, b_up, row(norm_pre), row(norm_post), row(head_norm_a), row(head_norm_b),
        w_ba, w_bb, w_o, w_pl, w_pg, row(norm_ple), amat, lvl2,
    )
    in_specs = [
        pl.BlockSpec((1, TL, D_MODEL), lambda b, t: (b, t, 0)),
        pl.BlockSpec((1, 1, TL, D_PLE), lambda b, t: (0, b, t, 0)),
    ] + [_const_spec(a.shape) for a in operands[2:]]

    return pl.pallas_call(
        _block_kernel,
        grid=(bsz, seq // TL),
        in_specs=in_specs,
        out_specs=pl.BlockSpec((1, TL, D_MODEL), lambda b, t: (b, t, 0)),
        out_shape=jax.ShapeDtypeStruct(x.shape, x.dtype),
        scratch_shapes=[
            pltpu.VMEM((TL, D_IN_PAD), _F32),
            pltpu.VMEM(((N_MXU_LEVELS + 1) * CH, D_BR), _F32),
            pltpu.VMEM(((N_MXU_LEVELS + 1) * CH, D_BR), _F32),
            pltpu.VMEM((2, CH, D_BR), _F32),
            pltpu.VMEM((2, CH, D_BR), _F32),
            pltpu.VMEM((N_OPERAND_ROWS, CH, D_BR), _BF16),
            pltpu.VMEM((N_OPERAND_ROWS, CH, D_BR), _BF16),
            pltpu.VMEM((N_HEADS // 2, CH, D_PAIR), _F32),
            pltpu.VMEM((N_HEADS // 2, CH, D_PAIR), _F32),
            pltpu.VMEM((N_HEADS, D_HEAD, D_HEAD), _F32),
            pltpu.VMEM((N_HEADS, D_HEAD, D_HEAD), _F32),
            pltpu.VMEM((TL, D_BR), _F32),
            pltpu.VMEM((TL, D_BR), _F32),
        ],
        compiler_params=pltpu.CompilerParams(
            dimension_semantics=("arbitrary", "arbitrary"),
            vmem_limit_bytes=V7X_VMEM_LIMIT_BYTES,
        ),
        name="hgrn2_gla_block",
    )(*operands)
```

```python
import math

import numpy as np
import jax
import jax.numpy as jnp
from jax import lax
from jax.experimental import pallas as pl
from jax.experimental.pallas import tpu as pltpu

D_MODEL = 1024
D_PLE = 256
N_HEADS = 4
D_HEAD = 128
D_BR = N_HEADS * D_HEAD
D_PAIR = 2 * D_HEAD
DK_A = 128
DK_B = 64
GLA_RANK = 16
GLA_TAU = 16.0
NORM_EPS = 1e-6
LOG2_E = 1.4426950408889634

SUBLANES = 8
BF16_ROWS = 16
N_MXU_LEVELS = SUBLANES.bit_length() - 1

CH = 128
N_LEVELS = CH.bit_length() - 1
TL = 512
RANK_PAD = 128
IN_WIDTHS = (D_BR, D_BR, D_BR, D_BR, N_HEADS * DK_B, N_HEADS * DK_B, D_BR, D_BR, GLA_RANK, D_MODEL, D_MODEL)

ROW_SUB_Q = 0
ROW_SUB_K = 1
ROW_QT = N_LEVELS
ROW_KH = N_LEVELS + 1
ROW_V = N_LEVELS + 2
N_OPERAND_ROWS = N_LEVELS + 3

OFF_QA = 0
OFF_FA = OFF_QA + D_BR
OFF_IA = OFF_FA + D_BR
OFF_GA = OFF_IA + D_BR
D_QK_B = N_HEADS * DK_B
OFF_QB = OFF_GA + D_BR
OFF_KB = OFF_QB + D_QK_B
OFF_VB = OFF_KB + D_QK_B
OFF_GB = OFF_VB + D_BR
OFF_AB = OFF_GB + D_BR
D_PROJ = OFF_AB + RANK_PAD
OFF_ZA = D_PROJ
OFF_ZB = OFF_ZA + D_MODEL
D_IN_PAD = OFF_ZB + D_MODEL

V7X_VMEM_LIMIT_BYTES = 56 * 1024 * 1024

_F32 = jnp.float32
_BF16 = jnp.bfloat16


def _dot(a, b):
    return jnp.dot(a, b, preferred_element_type=_F32)


def _dot_nt(a, b):
    return lax.dot_general(a, b, (((1,), (1,)), ((), ())), preferred_element_type=_F32)


def _dot_tn(a, b):
    return lax.dot_general(a, b, (((0,), (0,)), ((), ())), preferred_element_type=_F32)


def _rms(x, gain):
    ms = jnp.mean(x * x, axis=-1, keepdims=True)
    return x * lax.rsqrt(ms + NORM_EPS) * gain


def _sigmoid(x):
    return 0.5 * jnp.tanh(0.5 * x) + 0.5


def _silu(x):
    hx = 0.5 * x
    return hx * jnp.tanh(hx) + hx


def _block_diag(a, b):
    zero = jnp.zeros_like(a)
    return jnp.concatenate([jnp.concatenate([a, zero], axis=1), jnp.concatenate([zero, b], axis=1)], axis=0)


def _spread_heads(x):
    lane = lax.broadcasted_iota(jnp.int32, (x.shape[0], D_HEAD), 1)
    per_tile = D_HEAD // DK_B
    tiles = []
    for h in range(N_HEADS):
        t = x[:, (h // per_tile) * D_HEAD:(h // per_tile + 1) * D_HEAD]
        shift = (h % per_tile) * DK_B
        if shift:
            t = pltpu.roll(t, D_HEAD - shift, axis=1)
        tiles.append(jnp.where(lane < DK_B, t, 0.0))
    return jnp.concatenate(tiles, axis=1)


def _level_constants():
    t = np.arange(CH)
    mats = []
    for l in range(N_MXU_LEVELS):
        s = 1 << l
        ref = (t & ~(2 * s - 1)) + s - 1
        odd = (t & s) != 0
        u = t[None, :]
        a = np.where(odd[:, None], (u > ref[:, None]) & (u <= t[:, None]),
                     (u > t[:, None]) & (u <= ref[:, None]))
        mats.append(a)
    mats.append(t[None, :] <= t[:, None])
    a_all = np.concatenate(mats, axis=0).astype(np.float32)
    a_cat = np.concatenate([a_all, a_all], axis=1)
    x = t[:, None] ^ t[None, :]
    lvl = np.where(x > 0, np.floor(np.log2(np.maximum(x, 1))).astype(np.int32), -1)
    lvl = np.where(t[:, None] > t[None, :], lvl, -1).astype(np.int32)
    return a_cat, np.concatenate([lvl, lvl], axis=1)


MERGED_ROWS = BF16_ROWS
MERGED_LEVELS = MERGED_ROWS.bit_length() - 1
MERGED_MAX_EXP = 96.0
MERGED_MAX_KEY = 2.0 ** 24
N_PREP_STAGES = 1 + N_HEADS
N_PAIR_STAGES = N_HEADS // 2


def _gla_chunk(inputs_fn, v_fn, gate_fn, gain, amat_ref, lvl2, e_ref, qk_ref, w_ref, sc_ref, st_ref, o_ref, o_off,
               rows, merged):
    q, k, v, g = inputs_fn()
    qk_ref[0] = q
    qk_ref[1] = k
    w_ref[ROW_V] = v.astype(_BF16)
    g2 = g * LOG2_E
    g_hi = g2.astype(_BF16)
    g_lo = (g2 - g_hi.astype(_F32)).astype(_BF16)
    cum = slice(N_MXU_LEVELS * CH, (N_MXU_LEVELS + 1) * CH)
    need = cum if merged >= N_MXU_LEVELS else slice(0, cum.stop)
    e_ref[need] = _dot(amat_ref[need, :], jnp.concatenate([g_hi, g_lo], axis=0))
    yield

    sub = lax.broadcasted_iota(jnp.int32, (CH, D_HEAD), 0)
    odd_rows = [(sub & (1 << l)) != 0 for l in range(N_MXU_LEVELS)]
    d_last = []
    for ct in range(N_HEADS):
        cs = slice(ct * D_HEAD, (ct + 1) * D_HEAD)
        qc = qk_ref[0, :, cs]
        kc = qk_ref[1, :, cs]
        big_g = e_ref[cum, cs]
        if merged:
            blk = 1 << merged
            q_parts, k_parts = [], []
            for base in range(0, CH, blk):
                before = big_g[base - 1:base, :] if base else jnp.zeros((1, D_HEAD), _F32)
                rel = big_g[base:base + blk] - jnp.broadcast_to(before, (blk, D_HEAD))
                q_parts.append(qc[base:base + blk] * jnp.exp2(rel))
                k_parts.append(kc[base:base + blk] * jnp.exp2(-rel))
            w_ref[ROW_SUB_Q, :, cs] = jnp.concatenate(q_parts, axis=0).astype(_BF16)
            w_ref[ROW_SUB_K, :, cs] = jnp.concatenate(k_parts, axis=0).astype(_BF16)
        for l in range(merged, N_LEVELS):
            s = 1 << l
            if l < N_MXU_LEVELS:
                w = jnp.where(odd_rows[l], qc, kc) * jnp.exp2(e_ref[l * CH:(l + 1) * CH, cs])
            else:
                parts = []
                for base in range(0, CH, 2 * s):
                    mid = base + s
                    g_ref_row = jnp.broadcast_to(big_g[mid - 1:mid, :], (s, D_HEAD))
                    parts.append(kc[base:mid] * jnp.exp2(g_ref_row - big_g[base:mid]))
                    parts.append(qc[mid:mid + s] * jnp.exp2(big_g[mid:mid + s] - g_ref_row))
                w = jnp.concatenate(parts, axis=0)
            w_ref[l, :, cs] = w.astype(_BF16)
        g_last = big_g[CH - 1:CH, :]
        w_ref[ROW_QT, :, cs] = (qc * jnp.exp2(big_g)).astype(_BF16)
        w_ref[ROW_KH, :, cs] = (kc * jnp.exp2(g_last - big_g)).astype(_BF16)
        d_last.append(jnp.exp2(g_last))
        yield

    for pr in range(N_HEADS // 2):
        h0 = slice(pr * D_PAIR, pr * D_PAIR + D_HEAD)
        h1 = slice(pr * D_PAIR + D_HEAD, (pr + 1) * D_PAIR)
        ps = slice(pr * D_PAIR, (pr + 1) * D_PAIR)

        def keys(idx, lanes, s):
            if s < BF16_ROWS:
                return w_ref[idx, :, lanes]
            zero = jnp.zeros((s, D_HEAD), _BF16)
            return jnp.concatenate([zero if b % 2 else w_ref[idx, b * s:(b + 1) * s, lanes]
                                    for b in range(CH // s)], axis=0)

        low = jnp.zeros((CH, D_PAIR), _F32)
        if merged:
            r = _dot_nt(w_ref[ROW_SUB_Q, :, ps], _block_diag(w_ref[ROW_SUB_K, :, h0], w_ref[ROW_SUB_K, :, h1]))
            sc_ref[pr] = jnp.where((lvl2 >= 0) & (lvl2 < merged), r, low)
        for l in range(merged, N_LEVELS):
            s = 1 << l
            rhs = _block_diag(keys(l, h0, s), keys(l, h1, s))
            odd_blocks = [b for b in range(CH // s) if b % 2]
            if s >= BF16_ROWS:
                lhs = jnp.concatenate([w_ref[l, b * s:(b + 1) * s, ps] for b in odd_blocks], axis=0)
            else:
                lhs = w_ref[l, :, ps]
            r = _dot_nt(lhs, rhs)
            if s < SUBLANES:
                low = jnp.where(lvl2 == l, r, low)
                if 2 * s == SUBLANES:
                    sc_ref[pr] = low
                continue
            for i, b in enumerate(odd_blocks):
                src_row = i * s if s >= BF16_ROWS else b * s
                for head in range(2):
                    c0 = head * D_HEAD + (b - 1) * s
                    sc_ref[pr, b * s:(b + 1) * s, c0:c0 + s] = r[src_row:src_row + s, c0:c0 + s]
        sc = sc_ref[pr].astype(_BF16)

        st0 = st_ref[2 * pr]
        st1 = st_ref[2 * pr + 1]
        o = (_dot(sc, _block_diag(w_ref[ROW_V, :, h0], w_ref[ROW_V, :, h1]))
             + _dot_nt(w_ref[ROW_QT, :, ps], _block_diag(st0.astype(_BF16), st1.astype(_BF16))))
        st_ref[2 * pr] = st0 * d_last[2 * pr] + _dot_tn(w_ref[ROW_V, :, h0], w_ref[ROW_KH, :, h0])
        st_ref[2 * pr + 1] = st1 * d_last[2 * pr + 1] + _dot_tn(w_ref[ROW_V, :, h1], w_ref[ROW_KH, :, h1])
        for i, hs in enumerate((h0, h1)):
            self_score = jnp.sum(qk_ref[0, :, hs] * qk_ref[1, :, hs], axis=-1, keepdims=True)
            o_h = o[:, i * D_HEAD:(i + 1) * D_HEAD] + self_score * v_fn(hs)
            o_ref[rows, o_off + hs.start:o_off + hs.stop] = _rms(o_h, gain[:, hs]) * _silu(gate_fn(hs))
        yield


def _block_kernel(x_ref, p_ref, w_in_ref, lb_ref, w_up_ref, b_gla_ref, n_pre_ref, n_post_ref,
                  hn_a_ref, hn_b_ref, w_ba_ref, w_bb_ref, w_out_ref, w_ple_ref, w_pg_ref,
                  n_ple_ref, amat_ref, lvl_ref, out_ref,
                  proj_ref, gz_ref, e_ref, qka_ref, qkb_ref, wa_ref, wb_ref, sca_ref, scb_ref, st_a_ref, st_b_ref):
    @pl.when(pl.program_id(1) == 0)
    def _():
        st_a_ref[...] = jnp.zeros_like(st_a_ref)
        st_b_ref[...] = jnp.zeros_like(st_b_ref)

    x = x_ref[0]
    h = _rms(x, n_pre_ref[...]).astype(_BF16)
    proj_ref[...] = _dot_nt(h, w_in_ref[0:D_PROJ, :])
    gz_ref[...] = _sigmoid(_dot_nt(h, w_in_ref[D_PROJ:D_IN_PAD, :])).astype(_BF16)

    logits = lb_ref[...]
    ex = jnp.exp(logits - jnp.max(logits, axis=0, keepdims=True))
    lb = ex[0:1, :] / jnp.sum(ex, axis=0, keepdims=True)
    fg_half = 0.5 * (1.0 - lb)
    fg_mid = lb + fg_half
    lvl2 = lvl_ref[...]

    step_exp = MERGED_MAX_EXP / MERGED_ROWS
    z_limit = step_exp * GLA_TAU / LOG2_E - math.log(2.0)
    a_max = jnp.max(jnp.abs(proj_ref[:, OFF_AB:OFF_AB + RANK_PAD].astype(_BF16).astype(_F32)))
    up_max = jnp.max(jnp.sum(jnp.abs(w_up_ref[...].astype(_F32)), axis=0, keepdims=True))
    z_bound = a_max * up_max + jnp.max(jnp.abs(b_gla_ref[...]))
    safe = jnp.logical_and(
        jnp.logical_and(jnp.min(lb) > 2.0 ** -step_exp, z_bound < z_limit),
        jnp.max(jnp.abs(proj_ref[:, OFF_KB:OFF_KB + D_QK_B])) < MERGED_MAX_KEY)

    def chunk_phase(merged):
        streams = []
        for c in range(TL // CH):
            rows = slice(c * CH, (c + 1) * CH)

            def col(off, width=D_BR, rows=rows):
                return proj_ref[rows, off:off + width]

            def inputs_a(col=col):
                forget = fg_mid + fg_half * jnp.tanh(0.5 * col(OFF_FA))
                return _silu(col(OFF_QA)) * (DK_A ** -0.5), 1.0 - forget, col(OFF_IA), jnp.log(forget)

            def inputs_b(col=col):
                z = _dot(col(OFF_AB, RANK_PAD).astype(_BF16), w_up_ref[...]) + b_gla_ref[...]
                log_alpha = -(jnp.maximum(-z, 0.0) + jnp.log(1.0 + jnp.exp(-jnp.abs(z)))) / GLA_TAU
                q = _spread_heads(col(OFF_QB, D_QK_B)) * (DK_B ** -0.5)
                return q, _spread_heads(col(OFF_KB, D_QK_B)), col(OFF_VB), log_alpha

            def lanes_of(off, col=col):
                return lambda hs: col(off + hs.start, hs.stop - hs.start)

            streams.append(_gla_chunk(inputs_a, lanes_of(OFF_IA), lanes_of(OFF_GA), hn_a_ref[...], amat_ref, lvl2,
                                      e_ref, qka_ref, wa_ref, sca_ref, st_a_ref, proj_ref, OFF_QA, rows, merged))
            streams.append(_gla_chunk(inputs_b, lanes_of(OFF_VB), lanes_of(OFF_GB), hn_b_ref[...], amat_ref, lvl2,
                                      e_ref, qkb_ref, wb_ref, scb_ref, st_b_ref, proj_ref, OFF_QB, rows, merged))

        def advance(stream, n):
            for _ in range(n):
                next(stream, None)

        share = [N_PREP_STAGES // N_PAIR_STAGES + (j < N_PREP_STAGES % N_PAIR_STAGES)
                 for j in range(N_PAIR_STAGES)]
        advance(streams[0], N_PREP_STAGES)
        for i, cur in enumerate(streams):
            nxt = streams[i + 1] if i + 1 < len(streams) else None
            for j in range(N_PAIR_STAGES):
                if nxt is not None:
                    advance(nxt, 1)
                advance(cur, 1)
                if nxt is not None:
                    advance(nxt, share[j] - 1)

    pl.when(safe)(lambda: chunk_phase(MERGED_LEVELS))
    pl.when(jnp.logical_not(safe))(lambda: chunk_phase(0))

    ya = _dot(proj_ref[:, OFF_QA:OFF_QA + D_BR].astype(_BF16), w_ba_ref[...])
    yb = _dot(proj_ref[:, OFF_QB:OFF_QB + D_BR].astype(_BF16), w_bb_ref[...])
    y = gz_ref[:, 0:D_MODEL] * ya + gz_ref[:, D_MODEL:2 * D_MODEL] * yb
    y = _dot(y.astype(_BF16), w_out_ref[...])
    x1 = x + _rms(y, n_post_ref[...])
    e = _rms(_dot(p_ref[0, 0].astype(_BF16), w_ple_ref[...]), n_ple_ref[...])
    out_ref[0] = x1 + _sigmoid(_dot(x1.astype(_BF16), w_pg_ref[...])) * e


PREP_STEPS = 8


def _prep_kernel(w_in_t_ref, w_ba_ref, w_bb_ref, w_out_ref, w_ple_ref, w_pg_ref,
                 cat_ref, ba_ref, bb_ref, out_ref, ple_ref, pg_ref):
    cols = cat_ref.shape[1]

    def put(dst, src, n_rows):
        cat_ref[dst:dst + n_rows, :] = w_in_t_ref[src:src + n_rows, :].astype(_BF16)

    src_off = np.cumsum((0,) + IN_WIDTHS)
    qa, fa, ia, ga, qb, kb, vb, gb, ab, za, zb = (int(o) for o in src_off[:-1])
    for dst, src in ((OFF_QA, qa), (OFF_FA, fa), (OFF_IA, ia), (OFF_GA, ga), (OFF_VB, vb), (OFF_GB, gb)):
        put(dst, src, D_BR)
    put(OFF_ZA, za, D_MODEL)
    put(OFF_ZB, zb, D_MODEL)
    put(OFF_QB, qb, D_QK_B)
    put(OFF_KB, kb, D_QK_B)
    put(OFF_AB, ab, GLA_RANK)
    cat_ref[OFF_AB + GLA_RANK:OFF_AB + RANK_PAD, :] = jnp.zeros((RANK_PAD - GLA_RANK, cols), _BF16)

    for dst_ref, src_ref in ((ba_ref, w_ba_ref), (bb_ref, w_bb_ref), (out_ref, w_out_ref),
                             (ple_ref, w_ple_ref), (pg_ref, w_pg_ref)):
        dst_ref[...] = src_ref[0].astype(_BF16)


def _prep_weights(w_in_t, w_branch_a, w_branch_b, w_out, w_ple, w_ple_gate):
    rest = (w_branch_a, w_branch_b, w_out, w_ple, w_ple_gate)
    for w in rest:
        assert w.shape[0] == 1 and w.shape[1] % (PREP_STEPS * BF16_ROWS) == 0
    lane_block = D_MODEL // PREP_STEPS
    return pl.pallas_call(
        _prep_kernel,
        grid=(PREP_STEPS,),
        in_specs=[pl.BlockSpec((w_in_t.shape[0], lane_block), lambda i: (0, i))]
        + [pl.BlockSpec((1, w.shape[1] // PREP_STEPS, w.shape[2]), lambda i: (0, i, 0)) for w in rest],
        out_specs=[pl.BlockSpec((D_IN_PAD, lane_block), lambda i: (0, i))]
        + [pl.BlockSpec((w.shape[1] // PREP_STEPS, w.shape[2]), lambda i: (i, 0)) for w in rest],
        out_shape=[jax.ShapeDtypeStruct((D_IN_PAD, D_MODEL), _BF16)]
        + [jax.ShapeDtypeStruct(w.shape[1:], _BF16) for w in rest],
        compiler_params=pltpu.CompilerParams(dimension_semantics=("arbitrary",),
                                             vmem_limit_bytes=V7X_VMEM_LIMIT_BYTES),
        name="cast_weights",
    )(w_in_t, *rest)


def _pad_heads(w, dk):
    lead = w.shape[:-1]
    w = w.reshape(lead + (N_HEADS, dk))
    w = jnp.pad(w, [(0, 0)] * len(lead) + [(0, 0), (0, D_HEAD - dk)])
    return w.reshape(lead + (N_HEADS * D_HEAD,))


def _const_spec(shape):
    nd = len(shape)
    return pl.BlockSpec(shape, lambda b, t: (0,) * nd, pipeline_mode=pl.Buffered(1))


def kernel(x, p, w_in, lb_logits, w_gla_up, b_gla, norm_pre, norm_post, head_norm_a, head_norm_b,
           w_branch_a, w_branch_b, w_out, w_ple, w_ple_gate, norm_ple):
    bsz, seq, _ = x.shape
    assert x.shape == (bsz, seq, D_MODEL) and seq % TL == 0 and w_in.shape[0] == 1 and p.shape[0] == 1

    assert w_in.shape == (1, D_MODEL, sum(IN_WIDTHS))
    w_cat, w_ba, w_bb, w_o, w_pl, w_pg = _prep_weights(w_in[0].T, w_branch_a, w_branch_b, w_out, w_ple, w_ple_gate)
    w_up = jnp.pad(_pad_heads(w_gla_up[0], DK_B), ((0, RANK_PAD - GLA_RANK), (0, 0))).astype(_BF16)
    b_up = _pad_heads(b_gla[0], DK_B)[None, :]

    a_cat, lvl2 = _level_constants()
    amat = jnp.asarray(a_cat, dtype=_BF16)
    lvl2 = jnp.asarray(lvl2)

    row = lambda a: a[0][None, :]
    operands = (
        x, p, w_cat, lb_logits, w_up, b_up, row(norm_pre), row(norm_post), row(head_norm_a), row(head_norm_b),
        w_ba, w_bb, w_o, w_pl, w_pg, row(norm_ple), amat, lvl2,
    )
    in_specs = [
        pl.BlockSpec((1, TL, D_MODEL), lambda b, t: (b, t, 0)),
        pl.BlockSpec((1, 1, TL, D_PLE), lambda b, t: (0, b, t, 0)),
    ] + [_const_spec(a.shape) for a in operands[2:]]

    return pl.pallas_call(
        _block_kernel,
        grid=(bsz, seq // TL),
        in_specs=in_specs,
        out_specs=pl.BlockSpec((1, TL, D_MODEL), lambda b, t: (b, t, 0)),
        out_shape=jax.ShapeDtypeStruct(x.shape, x.dtype),
        scratch_shapes=[
            pltpu.VMEM((TL, D_PROJ), _F32),
            pltpu.VMEM((TL, 2 * D_MODEL), _BF16),
            pltpu.VMEM(((N_MXU_LEVELS + 1) * CH, D_BR), _F32),
            pltpu.VMEM((2, CH, D_BR), _F32),
            pltpu.VMEM((2, CH, D_BR), _F32),
            pltpu.VMEM((N_OPERAND_ROWS, CH, D_BR), _BF16),
            pltpu.VMEM((N_OPERAND_ROWS, CH, D_BR), _BF16),
            pltpu.VMEM((N_HEADS // 2, CH, D_PAIR), _F32),
            pltpu.VMEM((N_HEADS // 2, CH, D_PAIR), _F32),
            pltpu.VMEM((N_HEADS, D_HEAD, D_HEAD), _F32),
            pltpu.VMEM((N_HEADS, D_HEAD, D_HEAD), _F32),
        ],
        compiler_params=pltpu.CompilerParams(
            dimension_semantics=("arbitrary", "arbitrary"),
            vmem_limit_bytes=V7X_VMEM_LIMIT_BYTES,
        ),
        name="hgrn2_gla_block",
    )(*operands)
```

```python
import math

import numpy as np
import jax
import jax.numpy as jnp
from jax import lax
from jax.experimental import pallas as pl
from jax.experimental.pallas import tpu as pltpu

D_MODEL = 1024
D_PLE = 256
N_HEADS = 4
D_HEAD = 128
D_BR = N_HEADS * D_HEAD
D_PAIR = 2 * D_HEAD
DK_A = 128
DK_B = 64
GLA_RANK = 16
GLA_TAU = 16.0
NORM_EPS = 1e-6
LOG2_E = 1.4426950408889634

SUBLANES = 8
BF16_ROWS = 16
N_MXU_LEVELS = SUBLANES.bit_length() - 1

CH = 128
N_LEVELS = CH.bit_length() - 1
TL = 512
RANK_PAD = 128
IN_WIDTHS = (D_BR, D_BR, D_BR, D_BR, N_HEADS * DK_B, N_HEADS * DK_B, D_BR, D_BR, GLA_RANK, D_MODEL, D_MODEL)

ROW_SUB_Q = 0
ROW_SUB_K = 1
ROW_QT = N_LEVELS
ROW_KH = N_LEVELS + 1
ROW_V = N_LEVELS + 2
N_OPERAND_ROWS = N_LEVELS + 3

OFF_QA = 0
OFF_FA = OFF_QA + D_BR
OFF_IA = OFF_FA + D_BR
OFF_GA = OFF_IA + D_BR
D_QK_B = N_HEADS * DK_B
OFF_QB = OFF_GA + D_BR
OFF_KB = OFF_QB + D_QK_B
OFF_VB = OFF_KB + D_QK_B
OFF_GB = OFF_VB + D_BR
OFF_AB = OFF_GB + D_BR
D_PROJ = OFF_AB + RANK_PAD
OFF_ZA = D_PROJ
OFF_ZB = OFF_ZA + D_MODEL
D_IN_PAD = OFF_ZB + D_MODEL

V7X_VMEM_LIMIT_BYTES = 56 * 1024 * 1024

_F32 = jnp.float32
_BF16 = jnp.bfloat16


def _dot(a, b):
    return jnp.dot(a, b, preferred_element_type=_F32)


def _dot_nt(a, b):
    return lax.dot_general(a, b, (((1,), (1,)), ((), ())), preferred_element_type=_F32)


def _dot_tn(a, b):
    return lax.dot_general(a, b, (((0,), (0,)), ((), ())), preferred_element_type=_F32)


def _rms(x, gain):
    ms = jnp.mean(x * x, axis=-1, keepdims=True)
    return x * lax.rsqrt(ms + NORM_EPS) * gain


def _sigmoid(x):
    return 0.5 * jnp.tanh(0.5 * x) + 0.5


def _silu(x):
    hx = 0.5 * x
    return hx * jnp.tanh(hx) + hx


def _block_diag(a, b):
    zero = jnp.zeros_like(a)
    return jnp.concatenate([jnp.concatenate([a, zero], axis=1), jnp.concatenate([zero, b], axis=1)], axis=0)


def _spread_heads(x):
    lane = lax.broadcasted_iota(jnp.int32, (x.shape[0], D_HEAD), 1)
    per_tile = D_HEAD // DK_B
    tiles = []
    for h in range(N_HEADS):
        t = x[:, (h // per_tile) * D_HEAD:(h // per_tile + 1) * D_HEAD]
        shift = (h % per_tile) * DK_B
        if shift:
            t = pltpu.roll(t, D_HEAD - shift, axis=1)
        tiles.append(jnp.where(lane < DK_B, t, 0.0))
    return jnp.concatenate(tiles, axis=1)


def _level_constants():
    t = np.arange(CH)
    mats = []
    for l in range(N_MXU_LEVELS):
        s = 1 << l
        ref = (t & ~(2 * s - 1)) + s - 1
        odd = (t & s) != 0
        u = t[None, :]
        a = np.where(odd[:, None], (u > ref[:, None]) & (u <= t[:, None]),
                     (u > t[:, None]) & (u <= ref[:, None]))
        mats.append(a)
    mats.append(t[None, :] <= t[:, None])
    a_all = np.concatenate(mats, axis=0).astype(np.float32)
    a_cat = np.concatenate([a_all, a_all], axis=1)
    x = t[:, None] ^ t[None, :]
    lvl = np.where(x > 0, np.floor(np.log2(np.maximum(x, 1))).astype(np.int32), -1)
    lvl = np.where(t[:, None] > t[None, :], lvl, -1).astype(np.int32)
    return a_cat, np.concatenate([lvl, lvl], axis=1)


MERGED_ROWS = BF16_ROWS
MERGED_LEVELS = MERGED_ROWS.bit_length() - 1
MERGED_MAX_EXP = 96.0
MERGED_MAX_KEY = 2.0 ** 24
N_PREP_STAGES = 1 + N_HEADS
N_PAIR_STAGES = N_HEADS // 2


def _gla_chunk(inputs_fn, v_fn, gate_fn, gain, amat_ref, lvl2, e_ref, qk_ref, w_ref, sc_ref, st_ref, o_ref, o_off,
               rows, merged):
    q, k, v, g2 = inputs_fn()
    qk_ref[0] = q
    qk_ref[1] = k
    w_ref[ROW_V] = v.astype(_BF16)
    g_hi = g2.astype(_BF16)
    g_lo = (g2 - g_hi.astype(_F32)).astype(_BF16)
    cum = slice(N_MXU_LEVELS * CH, (N_MXU_LEVELS + 1) * CH)
    need = cum if merged >= N_MXU_LEVELS else slice(0, cum.stop)
    e_ref[need] = _dot(amat_ref[need, :], jnp.concatenate([g_hi, g_lo], axis=0))
    yield

    sub = lax.broadcasted_iota(jnp.int32, (CH, D_HEAD), 0)
    odd_rows = [(sub & (1 << l)) != 0 for l in range(N_MXU_LEVELS)]
    d_last = []
    for ct in range(N_HEADS):
        cs = slice(ct * D_HEAD, (ct + 1) * D_HEAD)
        qc = qk_ref[0, :, cs]
        kc = qk_ref[1, :, cs]
        big_g = e_ref[cum, cs]
        g_last = big_g[CH - 1:CH, :]
        if merged:
            blk = 1 << merged
            n_blk = CH // blk
            g_prev = [big_g[b * blk - 1:b * blk, :] if b else jnp.zeros((1, D_HEAD), _F32) for b in range(n_blk)]
            q_blk, k_blk = [], []
            for b in range(n_blk):
                rel = big_g[b * blk:(b + 1) * blk] - g_prev[b]
                q_blk.append(qc[b * blk:(b + 1) * blk] * jnp.exp2(rel))
                k_blk.append(kc[b * blk:(b + 1) * blk] * jnp.exp2(-rel))

            def put(row, parts):
                w_ref[row, :, cs] = jnp.concatenate(parts, axis=0).astype(_BF16)

            put(ROW_SUB_Q, q_blk)
            put(ROW_SUB_K, k_blk)
            for l in range(merged, N_LEVELS):
                per = (1 << l) // blk
                parts = []
                for b in range(n_blk):
                    first = (b // per) * per
                    if (b // per) % 2:
                        parts.append(q_blk[b] * jnp.exp2(g_prev[b] - g_prev[first]))
                    else:
                        parts.append(k_blk[b] * jnp.exp2(g_prev[first + per] - g_prev[b]))
                put(l, parts)
            put(ROW_QT, [q_blk[b] * jnp.exp2(g_prev[b]) for b in range(n_blk)])
            put(ROW_KH, [k_blk[b] * jnp.exp2(g_last - g_prev[b]) for b in range(n_blk)])
            d_last.append(jnp.exp2(g_last))
            yield
            continue
        for l in range(N_LEVELS):
            s = 1 << l
            if l < N_MXU_LEVELS:
                w = jnp.where(odd_rows[l], qc, kc) * jnp.exp2(e_ref[l * CH:(l + 1) * CH, cs])
            else:
                parts = []
                for base in range(0, CH, 2 * s):
                    mid = base + s
                    g_ref_row = jnp.broadcast_to(big_g[mid - 1:mid, :], (s, D_HEAD))
                    parts.append(kc[base:mid] * jnp.exp2(g_ref_row - big_g[base:mid]))
                    parts.append(qc[mid:mid + s] * jnp.exp2(big_g[mid:mid + s] - g_ref_row))
                w = jnp.concatenate(parts, axis=0)
            w_ref[l, :, cs] = w.astype(_BF16)
        w_ref[ROW_QT, :, cs] = (qc * jnp.exp2(big_g)).astype(_BF16)
        w_ref[ROW_KH, :, cs] = (kc * jnp.exp2(g_last - big_g)).astype(_BF16)
        d_last.append(jnp.exp2(g_last))
        yield

    for pr in range(N_HEADS // 2):
        h0 = slice(pr * D_PAIR, pr * D_PAIR + D_HEAD)
        h1 = slice(pr * D_PAIR + D_HEAD, (pr + 1) * D_PAIR)
        ps = slice(pr * D_PAIR, (pr + 1) * D_PAIR)

        def keys(idx, lanes, s):
            if s < BF16_ROWS:
                return w_ref[idx, :, lanes]
            zero = jnp.zeros((s, D_HEAD), _BF16)
            return jnp.concatenate([zero if b % 2 else w_ref[idx, b * s:(b + 1) * s, lanes]
                                    for b in range(CH // s)], axis=0)

        low = jnp.zeros((CH, D_PAIR), _F32)
        if merged:
            r = _dot_nt(w_ref[ROW_SUB_Q, :, ps], _block_diag(w_ref[ROW_SUB_K, :, h0], w_ref[ROW_SUB_K, :, h1]))
            sc_ref[pr] = jnp.where((lvl2 >= 0) & (lvl2 < merged), r, low)
        for l in range(merged, N_LEVELS):
            s = 1 << l
            rhs = _block_diag(keys(l, h0, s), keys(l, h1, s))
            odd_blocks = [b for b in range(CH // s) if b % 2]
            if s >= BF16_ROWS:
                lhs = jnp.concatenate([w_ref[l, b * s:(b + 1) * s, ps] for b in odd_blocks], axis=0)
            else:
                lhs = w_ref[l, :, ps]
            r = _dot_nt(lhs, rhs)
            if s < SUBLANES:
                low = jnp.where(lvl2 == l, r, low)
                if 2 * s == SUBLANES:
                    sc_ref[pr] = low
                continue
            for i, b in enumerate(odd_blocks):
                src_row = i * s if s >= BF16_ROWS else b * s
                for head in range(2):
                    c0 = head * D_HEAD + (b - 1) * s
                    sc_ref[pr, b * s:(b + 1) * s, c0:c0 + s] = r[src_row:src_row + s, c0:c0 + s]
        sc = sc_ref[pr].astype(_BF16)

        st0 = st_ref[2 * pr]
        st1 = st_ref[2 * pr + 1]
        o = (_dot(sc, _block_diag(w_ref[ROW_V, :, h0], w_ref[ROW_V, :, h1]))
             + _dot_nt(w_ref[ROW_QT, :, ps], _block_diag(st0.astype(_BF16), st1.astype(_BF16))))
        st_ref[2 * pr] = st0 * d_last[2 * pr] + _dot_tn(w_ref[ROW_V, :, h0], w_ref[ROW_KH, :, h0])
        st_ref[2 * pr + 1] = st1 * d_last[2 * pr + 1] + _dot_tn(w_ref[ROW_V, :, h1], w_ref[ROW_KH, :, h1])
        for i, hs in enumerate((h0, h1)):
            self_score = jnp.sum(qk_ref[0, :, hs] * qk_ref[1, :, hs], axis=-1, keepdims=True)
            o_h = o[:, i * D_HEAD:(i + 1) * D_HEAD] + self_score * v_fn(hs)
            o_ref[rows, o_off + hs.start:o_off + hs.stop] = _rms(o_h, gain[:, hs]) * _silu(gate_fn(hs))
        yield


def _block_kernel(x_ref, p_ref, w_in_ref, lb_ref, w_up_ref, b_gla_ref, n_pre_ref, n_post_ref,
                  hn_a_ref, hn_b_ref, w_ba_ref, w_bb_ref, w_out_ref, w_ple_ref, w_pg_ref,
                  n_ple_ref, amat_ref, lvl_ref, out_ref,
                  proj_ref, gz_ref, e_ref, qka_ref, qkb_ref, wa_ref, wb_ref, sca_ref, scb_ref, st_a_ref, st_b_ref):
    @pl.when(pl.program_id(1) == 0)
    def _():
        st_a_ref[...] = jnp.zeros_like(st_a_ref)
        st_b_ref[...] = jnp.zeros_like(st_b_ref)

    x = x_ref[0]
    h = _rms(x, n_pre_ref[...]).astype(_BF16)
    proj_ref[...] = _dot_nt(h, w_in_ref[0:D_PROJ, :])
    gz_ref[...] = _sigmoid(_dot_nt(h, w_in_ref[D_PROJ:D_IN_PAD, :])).astype(_BF16)

    logits = lb_ref[...]
    ex = jnp.exp(logits - jnp.max(logits, axis=0, keepdims=True))
    lb = ex[0:1, :] / jnp.sum(ex, axis=0, keepdims=True)
    fg_half = 0.5 * (1.0 - lb)
    fg_mid = lb + fg_half
    lvl2 = lvl_ref[...]

    step_exp = MERGED_MAX_EXP / MERGED_ROWS
    z_limit = step_exp * GLA_TAU / LOG2_E - math.log(2.0)
    a_max = jnp.max(jnp.abs(proj_ref[:, OFF_AB:OFF_AB + RANK_PAD].astype(_BF16).astype(_F32)))
    up_max = jnp.max(jnp.sum(jnp.abs(w_up_ref[...].astype(_F32)), axis=0, keepdims=True))
    z_bound = a_max * up_max + jnp.max(jnp.abs(b_gla_ref[...]))
    safe = jnp.logical_and(
        jnp.logical_and(jnp.min(lb) > 2.0 ** -step_exp, z_bound < z_limit),
        jnp.max(jnp.abs(proj_ref[:, OFF_KB:OFF_KB + D_QK_B])) < MERGED_MAX_KEY)

    def chunk_phase(merged):
        streams = []
        for c in range(TL // CH):
            rows = slice(c * CH, (c + 1) * CH)

            def col(off, width=D_BR, rows=rows):
                return proj_ref[rows, off:off + width]

            def inputs_a(col=col):
                forget = fg_mid + fg_half * jnp.tanh(0.5 * col(OFF_FA))
                return _silu(col(OFF_QA)) * (DK_A ** -0.5), 1.0 - forget, col(OFF_IA), jnp.log2(forget)

            def inputs_b(col=col):
                z = _dot(col(OFF_AB, RANK_PAD).astype(_BF16), w_up_ref[...]) + b_gla_ref[...]
                log2_alpha = (jnp.maximum(-z, 0.0) + jnp.log(1.0 + jnp.exp(-jnp.abs(z)))) * (-LOG2_E / GLA_TAU)
                q = _spread_heads(col(OFF_QB, D_QK_B)) * (DK_B ** -0.5)
                return q, _spread_heads(col(OFF_KB, D_QK_B)), col(OFF_VB), log2_alpha

            def lanes_of(off, col=col):
                return lambda hs: col(off + hs.start, hs.stop - hs.start)

            streams.append(_gla_chunk(inputs_a, lanes_of(OFF_IA), lanes_of(OFF_GA), hn_a_ref[...], amat_ref, lvl2,
                                      e_ref, qka_ref, wa_ref, sca_ref, st_a_ref, proj_ref, OFF_QA, rows, merged))
            streams.append(_gla_chunk(inputs_b, lanes_of(OFF_VB), lanes_of(OFF_GB), hn_b_ref[...], amat_ref, lvl2,
                                      e_ref, qkb_ref, wb_ref, scb_ref, st_b_ref, proj_ref, OFF_QB, rows, merged))

        def advance(stream, n):
            for _ in range(n):
                next(stream, None)

        share = [N_PREP_STAGES // N_PAIR_STAGES + (j < N_PREP_STAGES % N_PAIR_STAGES)
                 for j in range(N_PAIR_STAGES)]
        advance(streams[0], N_PREP_STAGES)
        for i, cur in enumerate(streams):
            nxt = streams[i + 1] if i + 1 < len(streams) else None
            for j in range(N_PAIR_STAGES):
                if nxt is not None:
                    advance(nxt, 1)
                advance(cur, 1)
                if nxt is not None:
                    advance(nxt, share[j] - 1)

    pl.when(safe)(lambda: chunk_phase(MERGED_LEVELS))
    pl.when(jnp.logical_not(safe))(lambda: chunk_phase(0))

    ya = _dot(proj_ref[:, OFF_QA:OFF_QA + D_BR].astype(_BF16), w_ba_ref[...])
    yb = _dot(proj_ref[:, OFF_QB:OFF_QB + D_BR].astype(_BF16), w_bb_ref[...])
    y = gz_ref[:, 0:D_MODEL] * ya + gz_ref[:, D_MODEL:2 * D_MODEL] * yb
    y = _dot(y.astype(_BF16), w_out_ref[...])
    x1 = x + _rms(y, n_post_ref[...])
    e = _rms(_dot(p_ref[0, 0].astype(_BF16), w_ple_ref[...]), n_ple_ref[...])
    out_ref[0] = x1 + _sigmoid(_dot(x1.astype(_BF16), w_pg_ref[...])) * e


PREP_STEPS = 8


def _prep_kernel(w_in_t_ref, w_ba_ref, w_bb_ref, w_out_ref, w_ple_ref, w_pg_ref,
                 cat_ref, ba_ref, bb_ref, out_ref, ple_ref, pg_ref):
    cols = cat_ref.shape[1]

    def put(dst, src, n_rows):
        cat_ref[dst:dst + n_rows, :] = w_in_t_ref[src:src + n_rows, :].astype(_BF16)

    src_off = np.cumsum((0,) + IN_WIDTHS)
    qa, fa, ia, ga, qb, kb, vb, gb, ab, za, zb = (int(o) for o in src_off[:-1])
    for dst, src in ((OFF_QA, qa), (OFF_FA, fa), (OFF_IA, ia), (OFF_GA, ga), (OFF_VB, vb), (OFF_GB, gb)):
        put(dst, src, D_BR)
    put(OFF_ZA, za, D_MODEL)
    put(OFF_ZB, zb, D_MODEL)
    put(OFF_QB, qb, D_QK_B)
    put(OFF_KB, kb, D_QK_B)
    put(OFF_AB, ab, GLA_RANK)
    cat_ref[OFF_AB + GLA_RANK:OFF_AB + RANK_PAD, :] = jnp.zeros((RANK_PAD - GLA_RANK, cols), _BF16)

    for dst_ref, src_ref in ((ba_ref, w_ba_ref), (bb_ref, w_bb_ref), (out_ref, w_out_ref),
                             (ple_ref, w_ple_ref), (pg_ref, w_pg_ref)):
        dst_ref[...] = src_ref[0].astype(_BF16)


def _prep_weights(w_in_t, w_branch_a, w_branch_b, w_out, w_ple, w_ple_gate):
    rest = (w_branch_a, w_branch_b, w_out, w_ple, w_ple_gate)
    for w in rest:
        assert w.shape[0] == 1 and w.shape[1] % (PREP_STEPS * BF16_ROWS) == 0
    lane_block = D_MODEL // PREP_STEPS
    return pl.pallas_call(
        _prep_kernel,
        grid=(PREP_STEPS,),
        in_specs=[pl.BlockSpec((w_in_t.shape[0], lane_block), lambda i: (0, i))]
        + [pl.BlockSpec((1, w.shape[1] // PREP_STEPS, w.shape[2]), lambda i: (0, i, 0)) for w in rest],
        out_specs=[pl.BlockSpec((D_IN_PAD, lane_block), lambda i: (0, i))]
        + [pl.BlockSpec((w.shape[1] // PREP_STEPS, w.shape[2]), lambda i: (i, 0)) for w in rest],
        out_shape=[jax.ShapeDtypeStruct((D_IN_PAD, D_MODEL), _BF16)]
        + [jax.ShapeDtypeStruct(w.shape[1:], _BF16) for w in rest],
        compiler_params=pltpu.CompilerParams(dimension_semantics=("arbitrary",),
                                             vmem_limit_bytes=V7X_VMEM_LIMIT_BYTES),
        name="cast_weights",
    )(w_in_t, *rest)


def _pad_heads(w, dk):
    lead = w.shape[:-1]
    w = w.reshape(lead + (N_HEADS, dk))
    w = jnp.pad(w, [(0, 0)] * len(lead) + [(0, 0), (0, D_HEAD - dk)])
    return w.reshape(lead + (N_HEADS * D_HEAD,))


def _const_spec(shape):
    nd = len(shape)
    return pl.BlockSpec(shape, lambda b, t: (0,) * nd, pipeline_mode=pl.Buffered(1))


def kernel(x, p, w_in, lb_logits, w_gla_up, b_gla, norm_pre, norm_post, head_norm_a, head_norm_b,
           w_branch_a, w_branch_b, w_out, w_ple, w_ple_gate, norm_ple):
    bsz, seq, _ = x.shape
    assert x.shape == (bsz, seq, D_MODEL) and seq % TL == 0 and w_in.shape[0] == 1 and p.shape[0] == 1

    assert w_in.shape == (1, D_MODEL, sum(IN_WIDTHS))
    w_cat, w_ba, w_bb, w_o, w_pl, w_pg = _prep_weights(w_in[0].T, w_branch_a, w_branch_b, w_out, w_ple, w_ple_gate)
    w_up = jnp.pad(_pad_heads(w_gla_up[0], DK_B), ((0, RANK_PAD - GLA_RANK), (0, 0))).astype(_BF16)
    b_up = _pad_heads(b_gla[0], DK_B)[None, :]

    a_cat, lvl2 = _level_constants()
    amat = jnp.asarray(a_cat, dtype=_BF16)
    lvl2 = jnp.asarray(lvl2)

    row = lambda a: a[0][None, :]
    operands = (
        x, p, w_cat, lb_logits, w_up, b_up, row(norm_pre), row(norm_post), row(head_norm_a), row(head_norm_b),
        w_ba, w_bb, w_o, w_pl, w_pg, row(norm_ple), amat, lvl2,
    )
    in_specs = [
        pl.BlockSpec((1, TL, D_MODEL), lambda b, t: (b, t, 0)),
        pl.BlockSpec((1, 1, TL, D_PLE), lambda b, t: (0, b, t, 0)),
    ] + [_const_spec(a.shape) for a in operands[2:]]

    return pl.pallas_call(
        _block_kernel,
        grid=(bsz, seq // TL),
        in_specs=in_specs,
        out_specs=pl.BlockSpec((1, TL, D_MODEL), lambda b, t: (b, t, 0)),
        out_shape=jax.ShapeDtypeStruct(x.shape, x.dtype),
        scratch_shapes=[
            pltpu.VMEM((TL, D_PROJ), _F32),
            pltpu.VMEM((TL, 2 * D_MODEL), _BF16),
            pltpu.VMEM(((N_MXU_LEVELS + 1) * CH, D_BR), _F32),
            pltpu.VMEM((2, CH, D_BR), _F32),
            pltpu.VMEM((2, CH, D_BR), _F32),
            pltpu.VMEM((N_OPERAND_ROWS, CH, D_BR), _BF16),
            pltpu.VMEM((N_OPERAND_ROWS, CH, D_BR), _BF16),
            pltpu.VMEM((N_HEADS // 2, CH, D_PAIR), _F32),
            pltpu.VMEM((N_HEADS // 2, CH, D_PAIR), _F32),
            pltpu.VMEM((N_HEADS, D_HEAD, D_HEAD), _F32),
            pltpu.VMEM((N_HEADS, D_HEAD, D_HEAD), _F32),
        ],
        compiler_params=pltpu.CompilerParams(
            dimension_semantics=("arbitrary", "arbitrary"),
            vmem_limit_bytes=V7X_VMEM_LIMIT_BYTES,
        ),
        name="hgrn2_gla_block",
    )(*operands)
```

```python
import math

import numpy as np
import jax
import jax.numpy as jnp
from jax import lax
from jax.experimental import pallas as pl
from jax.experimental.pallas import tpu as pltpu

D_MODEL = 1024
D_PLE = 256
N_HEADS = 4
D_HEAD = 128
D_BR = N_HEADS * D_HEAD
D_PAIR = 2 * D_HEAD
DK_A = 128
DK_B = 64
GLA_RANK = 16
GLA_TAU = 16.0
NORM_EPS = 1e-6
LOG2_E = 1.4426950408889634

SUBLANES = 8
BF16_ROWS = 16
N_MXU_LEVELS = SUBLANES.bit_length() - 1

CH = 128
N_LEVELS = CH.bit_length() - 1
TL = 512
RANK_PAD = 128
IN_WIDTHS = (D_BR, D_BR, D_BR, D_BR, N_HEADS * DK_B, N_HEADS * DK_B, D_BR, D_BR, GLA_RANK, D_MODEL, D_MODEL)

ROW_SUB_Q = 0
ROW_SUB_K = 1
ROW_QT = N_LEVELS
ROW_KH = N_LEVELS + 1
ROW_V = N_LEVELS + 2
N_OPERAND_ROWS = N_LEVELS + 3

OFF_QA = 0
OFF_FA = OFF_QA + D_BR
OFF_IA = OFF_FA + D_BR
OFF_GA = OFF_IA + D_BR
D_QK_B = N_HEADS * DK_B
OFF_QB = OFF_GA + D_BR
OFF_KB = OFF_QB + D_QK_B
OFF_VB = OFF_KB + D_QK_B
OFF_GB = OFF_VB + D_BR
OFF_AB = OFF_GB + D_BR
D_PROJ = OFF_AB + RANK_PAD
OFF_ZA = D_PROJ
OFF_ZB = OFF_ZA + D_MODEL
D_IN_PAD = OFF_ZB + D_MODEL

V7X_VMEM_LIMIT_BYTES = 57 * 1024 * 1024

_F32 = jnp.float32
_BF16 = jnp.bfloat16


def _dot(a, b):
    return jnp.dot(a, b, preferred_element_type=_F32)


def _dot_nt(a, b):
    return lax.dot_general(a, b, (((1,), (1,)), ((), ())), preferred_element_type=_F32)


def _dot_tn(a, b):
    return lax.dot_general(a, b, (((0,), (0,)), ((), ())), preferred_element_type=_F32)


def _rms(x, gain):
    n = x.shape[-1]
    ss = jnp.sum(x * x, axis=-1, keepdims=True)
    return x * lax.rsqrt(ss + n * NORM_EPS) * (gain * math.sqrt(n))


def _sigmoid(x):
    return 0.5 * jnp.tanh(0.5 * x) + 0.5


def _silu(x):
    hx = 0.5 * x
    return hx * jnp.tanh(hx) + hx


def _block_diag(a, b):
    zero = jnp.zeros_like(a)
    return jnp.concatenate([jnp.concatenate([a, zero], axis=1), jnp.concatenate([zero, b], axis=1)], axis=0)


def _spread_heads(x):
    lane = lax.broadcasted_iota(jnp.int32, (x.shape[0], D_HEAD), 1)
    per_tile = D_HEAD // DK_B
    tiles = []
    for h in range(N_HEADS):
        t = x[:, (h // per_tile) * D_HEAD:(h // per_tile + 1) * D_HEAD]
        shift = (h % per_tile) * DK_B
        if shift:
            t = pltpu.roll(t, D_HEAD - shift, axis=1)
        tiles.append(jnp.where(lane < DK_B, t, 0.0))
    return jnp.concatenate(tiles, axis=1)


def _level_constants():
    t = np.arange(CH)
    mats = []
    for l in range(N_MXU_LEVELS):
        s = 1 << l
        ref = (t & ~(2 * s - 1)) + s - 1
        odd = (t & s) != 0
        u = t[None, :]
        a = np.where(odd[:, None], (u > ref[:, None]) & (u <= t[:, None]),
                     (u > t[:, None]) & (u <= ref[:, None]))
        mats.append(a)
    mats.append(t[None, :] <= t[:, None])
    a_all = np.concatenate(mats, axis=0).astype(np.float32)
    a_cat = np.concatenate([a_all, a_all], axis=1)
    x = t[:, None] ^ t[None, :]
    lvl = np.where(x > 0, np.floor(np.log2(np.maximum(x, 1))).astype(np.int32), LVL_DIAG)
    lvl = np.where(t[:, None] >= t[None, :], lvl, LVL_NONE).astype(np.int32)
    return a_cat, np.concatenate([lvl, lvl], axis=1)


LVL_NONE = -1
LVL_DIAG = -2
MERGED_ROWS = BF16_ROWS
MERGED_LEVELS = MERGED_ROWS.bit_length() - 1
MERGED_MAX_EXP = 96.0
MERGED_MAX_KEY = 2.0 ** 24
GATE_PIECE = 512
N_PREP_STAGES = 1 + N_HEADS
N_PAIR_STAGES = N_HEADS // 2


def _gla_chunk(inputs_fn, v_fn, gate_fn, gain, amat_ref, lvl2, e_ref, qk_ref, w_ref, sc_ref, st_ref, o_ref, o_off,
               rows, merged):
    q, k, v, g2 = inputs_fn()
    qk_ref[0] = q
    qk_ref[1] = k
    w_ref[ROW_V] = v.astype(_BF16)
    g_hi = g2.astype(_BF16)
    g_lo = (g2 - g_hi.astype(_F32)).astype(_BF16)
    cum = slice(N_MXU_LEVELS * CH, (N_MXU_LEVELS + 1) * CH)
    need = cum if merged >= N_MXU_LEVELS else slice(0, cum.stop)
    e_ref[need] = _dot(amat_ref[need, :], jnp.concatenate([g_hi, g_lo], axis=0))
    yield

    sub = lax.broadcasted_iota(jnp.int32, (CH, D_HEAD), 0)
    odd_rows = [(sub & (1 << l)) != 0 for l in range(N_MXU_LEVELS)]
    d_last = []
    for ct in range(N_HEADS):
        cs = slice(ct * D_HEAD, (ct + 1) * D_HEAD)
        qc = qk_ref[0, :, cs]
        kc = qk_ref[1, :, cs]
        big_g = e_ref[cum, cs]
        g_last = big_g[CH - 1:CH, :]
        if merged:
            blk = 1 << merged
            n_blk = CH // blk
            g_prev = [big_g[b * blk - 1:b * blk, :] if b else jnp.zeros((1, D_HEAD), _F32) for b in range(n_blk)]
            q_blk, k_blk = [], []
            for b in range(n_blk):
                rel = big_g[b * blk:(b + 1) * blk] - g_prev[b]
                q_blk.append(qc[b * blk:(b + 1) * blk] * jnp.exp2(rel))
                k_blk.append(kc[b * blk:(b + 1) * blk] * jnp.exp2(-rel))

            def put(row, parts):
                w_ref[row, :, cs] = jnp.concatenate(parts, axis=0).astype(_BF16)

            put(ROW_SUB_Q, q_blk)
            put(ROW_SUB_K, k_blk)
            for l in range(merged, N_LEVELS):
                per = (1 << l) // blk
                parts = []
                for b in range(n_blk):
                    first = (b // per) * per
                    if (b // per) % 2:
                        parts.append(q_blk[b] * jnp.exp2(g_prev[b] - g_prev[first]))
                    else:
                        parts.append(k_blk[b] * jnp.exp2(g_prev[first + per] - g_prev[b]))
                put(l, parts)
            put(ROW_QT, [q_blk[b] * jnp.exp2(g_prev[b]) for b in range(n_blk)])
            put(ROW_KH, [k_blk[b] * jnp.exp2(g_last - g_prev[b]) for b in range(n_blk)])
            d_last.append(jnp.exp2(g_last))
            yield
            continue
        for l in range(N_LEVELS):
            s = 1 << l
            if l < N_MXU_LEVELS:
                w = jnp.where(odd_rows[l], qc, kc) * jnp.exp2(e_ref[l * CH:(l + 1) * CH, cs])
            else:
                parts = []
                for base in range(0, CH, 2 * s):
                    mid = base + s
                    g_ref_row = jnp.broadcast_to(big_g[mid - 1:mid, :], (s, D_HEAD))
                    parts.append(kc[base:mid] * jnp.exp2(g_ref_row - big_g[base:mid]))
                    parts.append(qc[mid:mid + s] * jnp.exp2(big_g[mid:mid + s] - g_ref_row))
                w = jnp.concatenate(parts, axis=0)
            w_ref[l, :, cs] = w.astype(_BF16)
        w_ref[ROW_QT, :, cs] = (qc * jnp.exp2(big_g)).astype(_BF16)
        w_ref[ROW_KH, :, cs] = (kc * jnp.exp2(g_last - big_g)).astype(_BF16)
        d_last.append(jnp.exp2(g_last))
        yield

    for pr in range(N_HEADS // 2):
        h0 = slice(pr * D_PAIR, pr * D_PAIR + D_HEAD)
        h1 = slice(pr * D_PAIR + D_HEAD, (pr + 1) * D_PAIR)
        ps = slice(pr * D_PAIR, (pr + 1) * D_PAIR)

        def keys(idx, lanes, s):
            if s < BF16_ROWS:
                return w_ref[idx, :, lanes]
            zero = jnp.zeros((s, D_HEAD), _BF16)
            return jnp.concatenate([zero if b % 2 else w_ref[idx, b * s:(b + 1) * s, lanes]
                                    for b in range(CH // s)], axis=0)

        low = jnp.zeros((CH, D_PAIR), _F32)
        if merged:
            r = _dot_nt(w_ref[ROW_SUB_Q, :, ps], _block_diag(w_ref[ROW_SUB_K, :, h0], w_ref[ROW_SUB_K, :, h1]))
            sc_ref[pr] = jnp.where((lvl2 != LVL_NONE) & (lvl2 < merged), r, low)
        for l in range(merged, N_LEVELS):
            s = 1 << l
            rhs = _block_diag(keys(l, h0, s), keys(l, h1, s))
            odd_blocks = [b for b in range(CH // s) if b % 2]
            if s >= BF16_ROWS:
                lhs = jnp.concatenate([w_ref[l, b * s:(b + 1) * s, ps] for b in odd_blocks], axis=0)
            else:
                lhs = w_ref[l, :, ps]
            r = _dot_nt(lhs, rhs)
            if s < SUBLANES:
                low = jnp.where(lvl2 == l, r, low)
                if 2 * s == SUBLANES:
                    sc_ref[pr] = low
                continue
            for i, b in enumerate(odd_blocks):
                src_row = i * s if s >= BF16_ROWS else b * s
                for head in range(2):
                    c0 = head * D_HEAD + (b - 1) * s
                    sc_ref[pr, b * s:(b + 1) * s, c0:c0 + s] = r[src_row:src_row + s, c0:c0 + s]
        sc = sc_ref[pr].astype(_BF16)

        st0 = st_ref[2 * pr]
        st1 = st_ref[2 * pr + 1]
        o = (_dot(sc, _block_diag(w_ref[ROW_V, :, h0], w_ref[ROW_V, :, h1]))
             + _dot_nt(w_ref[ROW_QT, :, ps], _block_diag(st0.astype(_BF16), st1.astype(_BF16))))
        st_ref[2 * pr] = st0 * d_last[2 * pr] + _dot_tn(w_ref[ROW_V, :, h0], w_ref[ROW_KH, :, h0])
        st_ref[2 * pr + 1] = st1 * d_last[2 * pr + 1] + _dot_tn(w_ref[ROW_V, :, h1], w_ref[ROW_KH, :, h1])
        for i, hs in enumerate((h0, h1)):
            o_h = o[:, i * D_HEAD:(i + 1) * D_HEAD]
            if not merged:
                self_score = jnp.sum(qk_ref[0, :, hs] * qk_ref[1, :, hs], axis=-1, keepdims=True)
                o_h = o_h + self_score * v_fn(hs)
            o_ref[rows, o_off + hs.start:o_off + hs.stop] = _rms(o_h, gain[:, hs]) * _silu(gate_fn(hs))
        yield


def _block_kernel(x_ref, p_ref, w_in_ref, lb_ref, w_up_ref, b_gla_ref, n_pre_ref, n_post_ref,
                  hn_a_ref, hn_b_ref, w_ba_ref, w_bb_ref, w_out_ref, w_ple_ref, w_pg_ref,
                  n_ple_ref, amat_ref, lvl_ref, out_ref,
                  proj_ref, gz_ref, e_ref, qka_ref, qkb_ref, wa_ref, wb_ref, sca_ref, scb_ref, st_a_ref, st_b_ref):
    @pl.when(pl.program_id(1) == 0)
    def _():
        st_a_ref[...] = jnp.zeros_like(st_a_ref)
        st_b_ref[...] = jnp.zeros_like(st_b_ref)

    x = x_ref[0]
    h = _rms(x, n_pre_ref[...]).astype(_BF16)
    proj_ref[...] = _dot_nt(h, w_in_ref[0:D_PROJ, :])

    def gate_pieces():
        def piece(a):
            def run():
                z = _dot_nt(h, w_in_ref[D_PROJ + a:D_PROJ + a + GATE_PIECE, :])
                gz_ref[:, a:a + GATE_PIECE] = _sigmoid(z).astype(_BF16)
            return run
        return [piece(a) for a in range(0, 2 * D_MODEL, GATE_PIECE)]

    logits = lb_ref[...]
    ex = jnp.exp(logits - jnp.max(logits, axis=0, keepdims=True))
    lb = ex[0:1, :] / jnp.sum(ex, axis=0, keepdims=True)
    fg_half = 0.5 * (1.0 - lb)
    fg_mid = lb + fg_half
    lvl2 = lvl_ref[...]

    step_exp = MERGED_MAX_EXP / MERGED_ROWS
    z_limit = step_exp * GLA_TAU / LOG2_E - math.log(2.0)
    a_max = jnp.max(jnp.abs(proj_ref[:, OFF_AB:OFF_AB + RANK_PAD].astype(_BF16).astype(_F32)))
    up_max = jnp.max(jnp.sum(jnp.abs(w_up_ref[...].astype(_F32)), axis=0, keepdims=True))
    z_bound = a_max * up_max + jnp.max(jnp.abs(b_gla_ref[...]))
    safe = jnp.logical_and(
        jnp.logical_and(jnp.min(lb) > 2.0 ** -step_exp, z_bound < z_limit),
        jnp.max(jnp.abs(proj_ref[:, OFF_KB:OFF_KB + D_QK_B])) < MERGED_MAX_KEY)

    def chunk_phase(merged):
        streams = []
        for c in range(TL // CH):
            rows = slice(c * CH, (c + 1) * CH)

            def col(off, width=D_BR, rows=rows):
                return proj_ref[rows, off:off + width]

            def inputs_a(col=col):
                forget = fg_mid + fg_half * jnp.tanh(0.5 * col(OFF_FA))
                return _silu(col(OFF_QA)) * (DK_A ** -0.5), 1.0 - forget, col(OFF_IA), jnp.log2(forget)

            def inputs_b(col=col):
                z = _dot(col(OFF_AB, RANK_PAD).astype(_BF16), w_up_ref[...]) + b_gla_ref[...]
                log2_alpha = (jnp.maximum(-z, 0.0) + jnp.log(1.0 + jnp.exp(-jnp.abs(z)))) * (-LOG2_E / GLA_TAU)
                q = _spread_heads(col(OFF_QB, D_QK_B)) * (DK_B ** -0.5)
                return q, _spread_heads(col(OFF_KB, D_QK_B)), col(OFF_VB), log2_alpha

            def lanes_of(off, col=col):
                return lambda hs: col(off + hs.start, hs.stop - hs.start)

            streams.append(_gla_chunk(inputs_a, lanes_of(OFF_IA), lanes_of(OFF_GA), hn_a_ref[...], amat_ref, lvl2,
                                      e_ref, qka_ref, wa_ref, sca_ref, st_a_ref, proj_ref, OFF_QA, rows, merged))
            streams.append(_gla_chunk(inputs_b, lanes_of(OFF_VB), lanes_of(OFF_GB), hn_b_ref[...], amat_ref, lvl2,
                                      e_ref, qkb_ref, wb_ref, scb_ref, st_b_ref, proj_ref, OFF_QB, rows, merged))

        def advance(stream, n):
            for _ in range(n):
                next(stream, None)

        share = [N_PREP_STAGES // N_PAIR_STAGES + (j < N_PREP_STAGES % N_PAIR_STAGES)
                 for j in range(N_PAIR_STAGES)]
        gates = gate_pieces()
        for j in range(max(N_PREP_STAGES, len(gates))):
            if j < len(gates):
                gates[j]()
            if j < N_PREP_STAGES:
                advance(streams[0], 1)
        for i, cur in enumerate(streams):
            nxt = streams[i + 1] if i + 1 < len(streams) else None
            for j in range(N_PAIR_STAGES):
                if nxt is not None:
                    advance(nxt, 1)
                advance(cur, 1)
                if nxt is not None:
                    advance(nxt, share[j] - 1)

    pl.when(safe)(lambda: chunk_phase(MERGED_LEVELS))
    pl.when(jnp.logical_not(safe))(lambda: chunk_phase(0))

    ya = _dot(proj_ref[:, OFF_QA:OFF_QA + D_BR].astype(_BF16), w_ba_ref[...])
    yb = _dot(proj_ref[:, OFF_QB:OFF_QB + D_BR].astype(_BF16), w_bb_ref[...])
    y = gz_ref[:, 0:D_MODEL] * ya + gz_ref[:, D_MODEL:2 * D_MODEL] * yb
    y = _dot(y.astype(_BF16), w_out_ref[...])
    x1 = x + _rms(y, n_post_ref[...])
    e = _rms(_dot(p_ref[0, 0].astype(_BF16), w_ple_ref[...]), n_ple_ref[...])
    out_ref[0] = x1 + _sigmoid(_dot(x1.astype(_BF16), w_pg_ref[...])) * e


PREP_STEPS = 8


def _prep_kernel(w_in_t_ref, w_ba_ref, w_bb_ref, w_out_ref, w_ple_ref, w_pg_ref,
                 cat_ref, ba_ref, bb_ref, out_ref, ple_ref, pg_ref):
    cols = cat_ref.shape[1]

    def put(dst, src, n_rows):
        cat_ref[dst:dst + n_rows, :] = w_in_t_ref[src:src + n_rows, :].astype(_BF16)

    src_off = np.cumsum((0,) + IN_WIDTHS)
    qa, fa, ia, ga, qb, kb, vb, gb, ab, za, zb = (int(o) for o in src_off[:-1])
    for dst, src in ((OFF_QA, qa), (OFF_FA, fa), (OFF_IA, ia), (OFF_GA, ga), (OFF_VB, vb), (OFF_GB, gb)):
        put(dst, src, D_BR)
    put(OFF_ZA, za, D_MODEL)
    put(OFF_ZB, zb, D_MODEL)
    put(OFF_QB, qb, D_QK_B)
    put(OFF_KB, kb, D_QK_B)
    put(OFF_AB, ab, GLA_RANK)
    cat_ref[OFF_AB + GLA_RANK:OFF_AB + RANK_PAD, :] = jnp.zeros((RANK_PAD - GLA_RANK, cols), _BF16)

    for dst_ref, src_ref in ((ba_ref, w_ba_ref), (bb_ref, w_bb_ref), (out_ref, w_out_ref),
                             (ple_ref, w_ple_ref), (pg_ref, w_pg_ref)):
        dst_ref[...] = src_ref[0].astype(_BF16)


def _prep_weights(w_in_t, w_branch_a, w_branch_b, w_out, w_ple, w_ple_gate):
    rest = (w_branch_a, w_branch_b, w_out, w_ple, w_ple_gate)
    for w in rest:
        assert w.shape[0] == 1 and w.shape[1] % (PREP_STEPS * BF16_ROWS) == 0
    lane_block = D_MODEL // PREP_STEPS
    return pl.pallas_call(
        _prep_kernel,
        grid=(PREP_STEPS,),
        in_specs=[pl.BlockSpec((w_in_t.shape[0], lane_block), lambda i: (0, i))]
        + [pl.BlockSpec((1, w.shape[1] // PREP_STEPS, w.shape[2]), lambda i: (0, i, 0)) for w in rest],
        out_specs=[pl.BlockSpec((D_IN_PAD, lane_block), lambda i: (0, i))]
        + [pl.BlockSpec((w.shape[1] // PREP_STEPS, w.shape[2]), lambda i: (i, 0)) for w in rest],
        out_shape=[jax.ShapeDtypeStruct((D_IN_PAD, D_MODEL), _BF16)]
        + [jax.ShapeDtypeStruct(w.shape[1:], _BF16) for w in rest],
        compiler_params=pltpu.CompilerParams(dimension_semantics=("arbitrary",),
                                             vmem_limit_bytes=V7X_VMEM_LIMIT_BYTES),
        name="cast_weights",
    )(w_in_t, *rest)


def _pad_heads(w, dk):
    lead = w.shape[:-1]
    w = w.reshape(lead + (N_HEADS, dk))
    w = jnp.pad(w, [(0, 0)] * len(lead) + [(0, 0), (0, D_HEAD - dk)])
    return w.reshape(lead + (N_HEADS * D_HEAD,))


def _const_spec(shape):
    nd = len(shape)
    return pl.BlockSpec(shape, lambda b, t: (0,) * nd, pipeline_mode=pl.Buffered(1))


def kernel(x, p, w_in, lb_logits, w_gla_up, b_gla, norm_pre, norm_post, head_norm_a, head_norm_b,
           w_branch_a, w_branch_b, w_out, w_ple, w_ple_gate, norm_ple):
    bsz, seq, _ = x.shape
    assert x.shape == (bsz, seq, D_MODEL) and seq % TL == 0 and w_in.shape[0] == 1 and p.shape[0] == 1

    assert w_in.shape == (1, D_MODEL, sum(IN_WIDTHS))
    w_cat, w_ba, w_bb, w_o, w_pl, w_pg = _prep_weights(w_in[0].T, w_branch_a, w_branch_b, w_out, w_ple, w_ple_gate)
    w_up = jnp.pad(_pad_heads(w_gla_up[0], DK_B), ((0, RANK_PAD - GLA_RANK), (0, 0))).astype(_BF16)
    b_up = _pad_heads(b_gla[0], DK_B)[None, :]

    a_cat, lvl2 = _level_constants()
    amat = jnp.asarray(a_cat, dtype=_BF16)
    lvl2 = jnp.asarray(lvl2)

    row = lambda a: a[0][None, :]
    operands = (
        x, p, w_cat, lb_logits, w_up, b_up, row(norm_pre), row(norm_post), row(head_norm_a), row(head_norm_b),
        w_ba, w_bb, w_o, w_pl, w_pg, row(norm_ple), amat, lvl2,
    )
    in_specs = [
        pl.BlockSpec((1, TL, D_MODEL), lambda b, t: (b, t, 0)),
        pl.BlockSpec((1, 1, TL, D_PLE), lambda b, t: (0, b, t, 0)),
    ] + [_const_spec(a.shape) for a in operands[2:]]

    return pl.pallas_call(
        _block_kernel,
        grid=(bsz, seq // TL),
        in_specs=in_specs,
        out_specs=pl.BlockSpec((1, TL, D_MODEL), lambda b, t: (b, t, 0)),
        out_shape=jax.ShapeDtypeStruct(x.shape, x.dtype),
        scratch_shapes=[
            pltpu.VMEM((TL, D_PROJ), _F32),
            pltpu.VMEM((TL, 2 * D_MODEL), _BF16),
            pltpu.VMEM(((N_MXU_LEVELS + 1) * CH, D_BR), _F32),
            pltpu.VMEM((2, CH, D_BR), _F32),
            pltpu.VMEM((2, CH, D_BR), _F32),
            pltpu.VMEM((N_OPERAND_ROWS, CH, D_BR), _BF16),
            pltpu.VMEM((N_OPERAND_ROWS, CH, D_BR), _BF16),
            pltpu.VMEM((N_HEADS // 2, CH, D_PAIR), _F32),
            pltpu.VMEM((N_HEADS // 2, CH, D_PAIR), _F32),
            pltpu.VMEM((N_HEADS, D_HEAD, D_HEAD), _F32),
            pltpu.VMEM((N_HEADS, D_HEAD, D_HEAD), _F32),
        ],
        compiler_params=pltpu.CompilerParams(
            dimension_semantics=("arbitrary", "arbitrary"),
            vmem_limit_bytes=V7X_VMEM_LIMIT_BYTES,
        ),
        name="hgrn2_gla_block",
    )(*operands)
```

```python
import math

import numpy as np
import jax
import jax.numpy as jnp
from jax import lax
from jax.experimental import pallas as pl
from jax.experimental.pallas import tpu as pltpu

D_MODEL = 1024
D_PLE = 256
N_HEADS = 4
D_HEAD = 128
D_BR = N_HEADS * D_HEAD
D_PAIR = 2 * D_HEAD
DK_A = 128
DK_B = 64
GLA_RANK = 16
GLA_TAU = 16.0
NORM_EPS = 1e-6
LOG2_E = 1.4426950408889634

SUBLANES = 8
BF16_ROWS = 16
N_MXU_LEVELS = SUBLANES.bit_length() - 1

CH = 128
N_LEVELS = CH.bit_length() - 1
TL = 512
RANK_PAD = 128
IN_WIDTHS = (D_BR, D_BR, D_BR, D_BR, N_HEADS * DK_B, N_HEADS * DK_B, D_BR, D_BR, GLA_RANK, D_MODEL, D_MODEL)

ROW_SUB_Q = 0
ROW_SUB_K = 1
ROW_QT = N_LEVELS
ROW_KH = N_LEVELS + 1
ROW_V = N_LEVELS + 2
N_OPERAND_ROWS = N_LEVELS + 3

OFF_QA = 0
OFF_FA = OFF_QA + D_BR
OFF_IA = OFF_FA + D_BR
OFF_GA = OFF_IA + D_BR
D_QK_B = N_HEADS * DK_B
OFF_QB = OFF_GA + D_BR
OFF_KB = OFF_QB + D_QK_B
OFF_VB = OFF_KB + D_QK_B
OFF_GB = OFF_VB + D_BR
OFF_AB = OFF_GB + D_BR
D_PROJ = OFF_AB + RANK_PAD
OFF_ZA = D_PROJ
OFF_ZB = OFF_ZA + D_MODEL
D_IN_PAD = OFF_ZB + D_MODEL

V7X_VMEM_LIMIT_BYTES = 57 * 1024 * 1024

_F32 = jnp.float32
_BF16 = jnp.bfloat16


def _dot(a, b):
    return jnp.dot(a, b, preferred_element_type=_F32)


def _dot_nt(a, b):
    return lax.dot_general(a, b, (((1,), (1,)), ((), ())), preferred_element_type=_F32)


def _dot_tn(a, b):
    return lax.dot_general(a, b, (((0,), (0,)), ((), ())), preferred_element_type=_F32)


def _rms(x, gain):
    n = x.shape[-1]
    ss = jnp.sum(x * x, axis=-1, keepdims=True)
    return x * lax.rsqrt(ss + n * NORM_EPS) * (gain * math.sqrt(n))


def _sigmoid(x):
    return 0.5 * jnp.tanh(0.5 * x) + 0.5


def _silu(x):
    hx = 0.5 * x
    return hx * jnp.tanh(hx) + hx


def _block_diag(a, b):
    zero = jnp.zeros_like(a)
    return jnp.concatenate([jnp.concatenate([a, zero], axis=1), jnp.concatenate([zero, b], axis=1)], axis=0)


def _spread_heads(x):
    lane = lax.broadcasted_iota(jnp.int32, (x.shape[0], D_HEAD), 1)
    per_tile = D_HEAD // DK_B
    tiles = []
    for h in range(N_HEADS):
        t = x[:, (h // per_tile) * D_HEAD:(h // per_tile + 1) * D_HEAD]
        shift = (h % per_tile) * DK_B
        if shift:
            t = pltpu.roll(t, D_HEAD - shift, axis=1)
        tiles.append(jnp.where(lane < DK_B, t, 0.0))
    return jnp.concatenate(tiles, axis=1)


def _level_constants():
    t = np.arange(CH)
    mats = []
    for l in range(N_MXU_LEVELS):
        s = 1 << l
        ref = (t & ~(2 * s - 1)) + s - 1
        odd = (t & s) != 0
        u = t[None, :]
        a = np.where(odd[:, None], (u > ref[:, None]) & (u <= t[:, None]),
                     (u > t[:, None]) & (u <= ref[:, None]))
        mats.append(a)
    mats.append(t[None, :] <= t[:, None])
    a_all = np.concatenate(mats, axis=0).astype(np.float32)
    a_cat = np.concatenate([a_all, a_all], axis=1)
    x = t[:, None] ^ t[None, :]
    lvl = np.where(x > 0, np.floor(np.log2(np.maximum(x, 1))).astype(np.int32), LVL_DIAG)
    lvl = np.where(t[:, None] >= t[None, :], lvl, LVL_NONE).astype(np.int32)
    return a_cat, np.concatenate([lvl, lvl], axis=1)


LVL_NONE = -1
LVL_DIAG = -2
MERGED_ROWS = BF16_ROWS
MERGED_LEVELS = MERGED_ROWS.bit_length() - 1
MERGED_MAX_EXP = 96.0
MERGED_MAX_KEY = 2.0 ** 24
EPILOGUE_PIECE = 512
GATE_PIECE = 512
N_PREP_STAGES = 1 + N_HEADS
N_PAIR_STAGES = N_HEADS // 2


def _gla_chunk(inputs_fn, v_fn, gate_fn, gain, amat_ref, lvl2, e_ref, qk_ref, w_ref, sc_ref, st_ref, o_ref, o_off,
               rows, merged):
    q, k, v, g2 = inputs_fn()
    qk_ref[0] = q
    qk_ref[1] = k
    w_ref[ROW_V] = v.astype(_BF16)
    g_hi = g2.astype(_BF16)
    g_lo = (g2 - g_hi.astype(_F32)).astype(_BF16)
    cum = slice(N_MXU_LEVELS * CH, (N_MXU_LEVELS + 1) * CH)
    need = cum if merged >= N_MXU_LEVELS else slice(0, cum.stop)
    e_ref[need] = _dot(amat_ref[need, :], jnp.concatenate([g_hi, g_lo], axis=0))
    yield

    sub = lax.broadcasted_iota(jnp.int32, (CH, D_HEAD), 0)
    odd_rows = [(sub & (1 << l)) != 0 for l in range(N_MXU_LEVELS)]
    d_last = []
    for ct in range(N_HEADS):
        cs = slice(ct * D_HEAD, (ct + 1) * D_HEAD)
        qc = qk_ref[0, :, cs]
        kc = qk_ref[1, :, cs]
        big_g = e_ref[cum, cs]
        g_last = big_g[CH - 1:CH, :]
        if merged:
            blk = 1 << merged
            n_blk = CH // blk
            g_prev = [big_g[b * blk - 1:b * blk, :] if b else jnp.zeros((1, D_HEAD), _F32) for b in range(n_blk)]
            q_blk, k_blk = [], []
            for b in range(n_blk):
                rel = big_g[b * blk:(b + 1) * blk] - g_prev[b]
                q_blk.append(qc[b * blk:(b + 1) * blk] * jnp.exp2(rel))
                k_blk.append(kc[b * blk:(b + 1) * blk] * jnp.exp2(-rel))

            def put(row, parts):
                w_ref[row, :, cs] = jnp.concatenate(parts, axis=0).astype(_BF16)

            put(ROW_SUB_Q, q_blk)
            put(ROW_SUB_K, k_blk)
            for l in range(merged, N_LEVELS):
                per = (1 << l) // blk
                parts = []
                for b in range(n_blk):
                    first = (b // per) * per
                    if (b // per) % 2:
                        parts.append(q_blk[b] * jnp.exp2(g_prev[b] - g_prev[first]))
                    else:
                        parts.append(k_blk[b] * jnp.exp2(g_prev[first + per] - g_prev[b]))
                put(l, parts)
            put(ROW_QT, [q_blk[b] * jnp.exp2(g_prev[b]) for b in range(n_blk)])
            put(ROW_KH, [k_blk[b] * jnp.exp2(g_last - g_prev[b]) for b in range(n_blk)])
            d_last.append(jnp.exp2(g_last))
            yield
            continue
        for l in range(N_LEVELS):
            s = 1 << l
            if l < N_MXU_LEVELS:
                w = jnp.where(odd_rows[l], qc, kc) * jnp.exp2(e_ref[l * CH:(l + 1) * CH, cs])
            else:
                parts = []
                for base in range(0, CH, 2 * s):
                    mid = base + s
                    g_ref_row = jnp.broadcast_to(big_g[mid - 1:mid, :], (s, D_HEAD))
                    parts.append(kc[base:mid] * jnp.exp2(g_ref_row - big_g[base:mid]))
                    parts.append(qc[mid:mid + s] * jnp.exp2(big_g[mid:mid + s] - g_ref_row))
                w = jnp.concatenate(parts, axis=0)
            w_ref[l, :, cs] = w.astype(_BF16)
        w_ref[ROW_QT, :, cs] = (qc * jnp.exp2(big_g)).astype(_BF16)
        w_ref[ROW_KH, :, cs] = (kc * jnp.exp2(g_last - big_g)).astype(_BF16)
        d_last.append(jnp.exp2(g_last))
        yield

    for pr in range(N_HEADS // 2):
        h0 = slice(pr * D_PAIR, pr * D_PAIR + D_HEAD)
        h1 = slice(pr * D_PAIR + D_HEAD, (pr + 1) * D_PAIR)
        ps = slice(pr * D_PAIR, (pr + 1) * D_PAIR)

        def keys(idx, lanes, s):
            if s < BF16_ROWS:
                return w_ref[idx, :, lanes]
            zero = jnp.zeros((s, D_HEAD), _BF16)
            return jnp.concatenate([zero if b % 2 else w_ref[idx, b * s:(b + 1) * s, lanes]
                                    for b in range(CH // s)], axis=0)

        low = jnp.zeros((CH, D_PAIR), _F32)
        if merged:
            r = _dot_nt(w_ref[ROW_SUB_Q, :, ps], _block_diag(w_ref[ROW_SUB_K, :, h0], w_ref[ROW_SUB_K, :, h1]))
            sc_ref[pr] = jnp.where((lvl2 != LVL_NONE) & (lvl2 < merged), r, low)
        for l in range(merged, N_LEVELS):
            s = 1 << l
            rhs = _block_diag(keys(l, h0, s), keys(l, h1, s))
            odd_blocks = [b for b in range(CH // s) if b % 2]
            if s >= BF16_ROWS:
                lhs = jnp.concatenate([w_ref[l, b * s:(b + 1) * s, ps] for b in odd_blocks], axis=0)
            else:
                lhs = w_ref[l, :, ps]
            r = _dot_nt(lhs, rhs)
            if s < SUBLANES:
                low = jnp.where(lvl2 == l, r, low)
                if 2 * s == SUBLANES:
                    sc_ref[pr] = low
                continue
            for i, b in enumerate(odd_blocks):
                src_row = i * s if s >= BF16_ROWS else b * s
                for head in range(2):
                    c0 = head * D_HEAD + (b - 1) * s
                    sc_ref[pr, b * s:(b + 1) * s, c0:c0 + s] = r[src_row:src_row + s, c0:c0 + s]
        sc = sc_ref[pr].astype(_BF16)

        st0 = st_ref[2 * pr]
        st1 = st_ref[2 * pr + 1]
        o = (_dot(sc, _block_diag(w_ref[ROW_V, :, h0], w_ref[ROW_V, :, h1]))
             + _dot_nt(w_ref[ROW_QT, :, ps], _block_diag(st0.astype(_BF16), st1.astype(_BF16))))
        st_ref[2 * pr] = st0 * d_last[2 * pr] + _dot_tn(w_ref[ROW_V, :, h0], w_ref[ROW_KH, :, h0])
        st_ref[2 * pr + 1] = st1 * d_last[2 * pr + 1] + _dot_tn(w_ref[ROW_V, :, h1], w_ref[ROW_KH, :, h1])
        for i, hs in enumerate((h0, h1)):
            o_h = o[:, i * D_HEAD:(i + 1) * D_HEAD]
            if not merged:
                self_score = jnp.sum(qk_ref[0, :, hs] * qk_ref[1, :, hs], axis=-1, keepdims=True)
                o_h = o_h + self_score * v_fn(hs)
            o_ref[rows, o_off + hs.start:o_off + hs.stop] = _rms(o_h, gain[:, hs]) * _silu(gate_fn(hs))
        yield


def _block_kernel(x_ref, p_ref, w_in_ref, lb_ref, w_up_ref, b_gla_ref, n_pre_ref, n_post_ref,
                  hn_a_ref, hn_b_ref, w_ba_ref, w_bb_ref, w_out_ref, w_ple_ref, w_pg_ref,
                  n_ple_ref, amat_ref, lvl_ref, out_ref,
                  proj_ref, gz_ref, e_ref, qka_ref, qkb_ref, wa_ref, wb_ref, sca_ref, scb_ref, st_a_ref, st_b_ref):
    @pl.when(pl.program_id(1) == 0)
    def _():
        st_a_ref[...] = jnp.zeros_like(st_a_ref)
        st_b_ref[...] = jnp.zeros_like(st_b_ref)

    x = x_ref[0]
    h = _rms(x, n_pre_ref[...]).astype(_BF16)
    proj_ref[:, OFF_QB:D_PROJ] = _dot_nt(h, w_in_ref[OFF_QB:D_PROJ, :])

    def gate_pieces():
        def piece(a):
            def run():
                z = _dot_nt(h, w_in_ref[D_PROJ + a:D_PROJ + a + GATE_PIECE, :])
                gz_ref[:, a:a + GATE_PIECE] = _sigmoid(z).astype(_BF16)
            return run
        return [piece(a) for a in range(0, 2 * D_MODEL, GATE_PIECE)]

    logits = lb_ref[...]
    ex = jnp.exp(logits - jnp.max(logits, axis=0, keepdims=True))
    lb = ex[0:1, :] / jnp.sum(ex, axis=0, keepdims=True)
    fg_half = 0.5 * (1.0 - lb)
    fg_mid = lb + fg_half
    lvl2 = lvl_ref[...]

    step_exp = MERGED_MAX_EXP / MERGED_ROWS
    z_limit = step_exp * GLA_TAU / LOG2_E - math.log(2.0)
    a_max = jnp.max(jnp.abs(proj_ref[:, OFF_AB:OFF_AB + RANK_PAD].astype(_BF16).astype(_F32)))
    up_max = jnp.max(jnp.sum(jnp.abs(w_up_ref[...].astype(_F32)), axis=0, keepdims=True))
    z_bound = a_max * up_max + jnp.max(jnp.abs(b_gla_ref[...]))
    safe = jnp.logical_and(
        jnp.logical_and(jnp.min(lb) > 2.0 ** -step_exp, z_bound < z_limit),
        jnp.max(jnp.abs(proj_ref[:, OFF_KB:OFF_KB + D_QK_B])) < MERGED_MAX_KEY)

    def chunk_phase(merged):
        streams = []
        for c in range(TL // CH):
            rows = slice(c * CH, (c + 1) * CH)

            def col(off, width=D_BR, rows=rows):
                return proj_ref[rows, off:off + width]

            def inputs_a(col=col):
                forget = fg_mid + fg_half * jnp.tanh(0.5 * col(OFF_FA))
                return _silu(col(OFF_QA)) * (DK_A ** -0.5), 1.0 - forget, col(OFF_IA), jnp.log2(forget)

            def inputs_b(col=col):
                z = _dot(col(OFF_AB, RANK_PAD).astype(_BF16), w_up_ref[...]) + b_gla_ref[...]
                log2_alpha = (jnp.maximum(-z, 0.0) + jnp.log(1.0 + jnp.exp(-jnp.abs(z)))) * (-LOG2_E / GLA_TAU)
                q = _spread_heads(col(OFF_QB, D_QK_B)) * (DK_B ** -0.5)
                return q, _spread_heads(col(OFF_KB, D_QK_B)), col(OFF_VB), log2_alpha

            def lanes_of(off, col=col):
                return lambda hs: col(off + hs.start, hs.stop - hs.start)

            streams.append(_gla_chunk(inputs_a, lanes_of(OFF_IA), lanes_of(OFF_GA), hn_a_ref[...], amat_ref, lvl2,
                                      e_ref, qka_ref, wa_ref, sca_ref, st_a_ref, proj_ref, OFF_QA, rows, merged))
            streams.append(_gla_chunk(inputs_b, lanes_of(OFF_VB), lanes_of(OFF_GB), hn_b_ref[...], amat_ref, lvl2,
                                      e_ref, qkb_ref, wb_ref, scb_ref, st_b_ref, proj_ref, OFF_QB, rows, merged))

        def advance(stream, n):
            for _ in range(n):
                next(stream, None)

        share = [N_PREP_STAGES // N_PAIR_STAGES + (j < N_PREP_STAGES % N_PAIR_STAGES)
                 for j in range(N_PAIR_STAGES)]
        gates = gate_pieces()
        for j in range(max(N_PREP_STAGES, len(gates))):
            if j < len(gates):
                gates[j]()
            if j < N_PREP_STAGES:
                advance(streams[0], 1)
        for i, cur in enumerate(streams):
            nxt = streams[i + 1] if i + 1 < len(streams) else None
            for j in range(N_PAIR_STAGES):
                if nxt is not None:
                    advance(nxt, 1)
                advance(cur, 1)
                if nxt is not None:
                    advance(nxt, share[j] - 1)

    proj_ref[:, 0:OFF_QB] = _dot_nt(h, w_in_ref[0:OFF_QB, :])

    pl.when(safe)(lambda: chunk_phase(MERGED_LEVELS))
    pl.when(jnp.logical_not(safe))(lambda: chunk_phase(0))

    ya = _dot(proj_ref[:, OFF_QA:OFF_QA + D_BR].astype(_BF16), w_ba_ref[...])
    yb = _dot(proj_ref[:, OFF_QB:OFF_QB + D_BR].astype(_BF16), w_bb_ref[...])
    y = gz_ref[:, 0:D_MODEL] * ya + gz_ref[:, D_MODEL:2 * D_MODEL] * yb
    y = _dot(y.astype(_BF16), w_out_ref[...])
    x1 = x + _rms(y, n_post_ref[...])
    e = _rms(_dot(p_ref[0, 0].astype(_BF16), w_ple_ref[...]), n_ple_ref[...])
    x1b = x1.astype(_BF16)
    for c in range(0, D_MODEL, EPILOGUE_PIECE):
        cols = slice(c, c + EPILOGUE_PIECE)
        out_ref[0, :, cols] = x1[:, cols] + _sigmoid(_dot(x1b, w_pg_ref[:, cols])) * e[:, cols]


PREP_STEPS = 8


def _prep_kernel(w_in_t_ref, w_ba_ref, w_bb_ref, w_out_ref, w_ple_ref, w_pg_ref,
                 cat_ref, ba_ref, bb_ref, out_ref, ple_ref, pg_ref):
    cols = cat_ref.shape[1]

    def put(dst, src, n_rows):
        cat_ref[dst:dst + n_rows, :] = w_in_t_ref[src:src + n_rows, :].astype(_BF16)

    src_off = np.cumsum((0,) + IN_WIDTHS)
    qa, fa, ia, ga, qb, kb, vb, gb, ab, za, zb = (int(o) for o in src_off[:-1])
    for dst, src in ((OFF_QA, qa), (OFF_FA, fa), (OFF_IA, ia), (OFF_GA, ga), (OFF_VB, vb), (OFF_GB, gb)):
        put(dst, src, D_BR)
    put(OFF_ZA, za, D_MODEL)
    put(OFF_ZB, zb, D_MODEL)
    put(OFF_QB, qb, D_QK_B)
    put(OFF_KB, kb, D_QK_B)
    put(OFF_AB, ab, GLA_RANK)
    cat_ref[OFF_AB + GLA_RANK:OFF_AB + RANK_PAD, :] = jnp.zeros((RANK_PAD - GLA_RANK, cols), _BF16)

    for dst_ref, src_ref in ((ba_ref, w_ba_ref), (bb_ref, w_bb_ref), (out_ref, w_out_ref),
                             (ple_ref, w_ple_ref), (pg_ref, w_pg_ref)):
        dst_ref[...] = src_ref[0].astype(_BF16)


def _prep_weights(w_in_t, w_branch_a, w_branch_b, w_out, w_ple, w_ple_gate):
    rest = (w_branch_a, w_branch_b, w_out, w_ple, w_ple_gate)
    for w in rest:
        assert w.shape[0] == 1 and w.shape[1] % (PREP_STEPS * BF16_ROWS) == 0
    lane_block = D_MODEL // PREP_STEPS
    return pl.pallas_call(
        _prep_kernel,
        grid=(PREP_STEPS,),
        in_specs=[pl.BlockSpec((w_in_t.shape[0], lane_block), lambda i: (0, i))]
        + [pl.BlockSpec((1, w.shape[1] // PREP_STEPS, w.shape[2]), lambda i: (0, i, 0)) for w in rest],
        out_specs=[pl.BlockSpec((D_IN_PAD, lane_block), lambda i: (0, i))]
        + [pl.BlockSpec((w.shape[1] // PREP_STEPS, w.shape[2]), lambda i: (i, 0)) for w in rest],
        out_shape=[jax.ShapeDtypeStruct((D_IN_PAD, D_MODEL), _BF16)]
        + [jax.ShapeDtypeStruct(w.shape[1:], _BF16) for w in rest],
        compiler_params=pltpu.CompilerParams(dimension_semantics=("arbitrary",),
                                             vmem_limit_bytes=V7X_VMEM_LIMIT_BYTES),
        name="cast_weights",
    )(w_in_t, *rest)


def _pad_heads(w, dk):
    lead = w.shape[:-1]
    w = w.reshape(lead + (N_HEADS, dk))
    w = jnp.pad(w, [(0, 0)] * len(lead) + [(0, 0), (0, D_HEAD - dk)])
    return w.reshape(lead + (N_HEADS * D_HEAD,))


def _const_spec(shape):
    nd = len(shape)
    return pl.BlockSpec(shape, lambda b, t: (0,) * nd, pipeline_mode=pl.Buffered(1))


def kernel(x, p, w_in, lb_logits, w_gla_up, b_gla, norm_pre, norm_post, head_norm_a, head_norm_b,
           w_branch_a, w_branch_b, w_out, w_ple, w_ple_gate, norm_ple):
    bsz, seq, _ = x.shape
    assert x.shape == (bsz, seq, D_MODEL) and seq % TL == 0 and w_in.shape[0] == 1 and p.shape[0] == 1

    assert w_in.shape == (1, D_MODEL, sum(IN_WIDTHS))
    w_cat, w_ba, w_bb, w_o, w_pl, w_pg = _prep_weights(w_in[0].T, w_branch_a, w_branch_b, w_out, w_ple, w_ple_gate)
    w_up = jnp.pad(_pad_heads(w_gla_up[0], DK_B), ((0, RANK_PAD - GLA_RANK), (0, 0))).astype(_BF16)
    b_up = _pad_heads(b_gla[0], DK_B)[None, :]

    a_cat, lvl2 = _level_constants()
    amat = jnp.asarray(a_cat, dtype=_BF16)
    lvl2 = jnp.asarray(lvl2)

    row = lambda a: a[0][None, :]
    operands = (
        x, p, w_cat, lb_logits, w_up, b_up, row(norm_pre), row(norm_post), row(head_norm_a), row(head_norm_b),
        w_ba, w_bb, w_o, w_pl, w_pg, row(norm_ple), amat, lvl2,
    )
    in_specs = [
        pl.BlockSpec((1, TL, D_MODEL), lambda b, t: (b, t, 0)),
        pl.BlockSpec((1, 1, TL, D_PLE), lambda b, t: (0, b, t, 0)),
    ] + [_const_spec(a.shape) for a in operands[2:]]

    return pl.pallas_call(
        _block_kernel,
        grid=(bsz, seq // TL),
        in_specs=in_specs,
        out_specs=pl.BlockSpec((1, TL, D_MODEL), lambda b, t: (b, t, 0)),
        out_shape=jax.ShapeDtypeStruct(x.shape, x.dtype),
        scratch_shapes=[
            pltpu.VMEM((TL, D_PROJ), _F32),
            pltpu.VMEM((TL, 2 * D_MODEL), _BF16),
            pltpu.VMEM(((N_MXU_LEVELS + 1) * CH, D_BR), _F32),
            pltpu.VMEM((2, CH, D_BR), _F32),
            pltpu.VMEM((2, CH, D_BR), _F32),
            pltpu.VMEM((N_OPERAND_ROWS, CH, D_BR), _BF16),
            pltpu.VMEM((N_OPERAND_ROWS, CH, D_BR), _BF16),
            pltpu.VMEM((N_HEADS // 2, CH, D_PAIR), _F32),
            pltpu.VMEM((N_HEADS // 2, CH, D_PAIR), _F32),
            pltpu.VMEM((N_HEADS, D_HEAD, D_HEAD), _F32),
            pltpu.VMEM((N_HEADS, D_HEAD, D_HEAD), _F32),
        ],
        compiler_params=pltpu.CompilerParams(
            dimension_semantics=("arbitrary", "arbitrary"),
            vmem_limit_bytes=V7X_VMEM_LIMIT_BYTES,
        ),
        name="hgrn2_gla_block",
    )(*operands)
```

```python
import math

import numpy as np
import jax
import jax.numpy as jnp
from jax import lax
from jax.experimental import pallas as pl
from jax.experimental.pallas import tpu as pltpu

D_MODEL = 1024
D_PLE = 256
N_HEADS = 4
D_HEAD = 128
D_BR = N_HEADS * D_HEAD
D_PAIR = 2 * D_HEAD
DK_A = 128
DK_B = 64
GLA_RANK = 16
GLA_TAU = 16.0
NORM_EPS = 1e-6
LOG2_E = 1.4426950408889634

SUBLANES = 8
BF16_ROWS = 16
N_MXU_LEVELS = SUBLANES.bit_length() - 1

CH = 128
N_LEVELS = CH.bit_length() - 1
TL = 512
RANK_PAD = 128
IN_WIDTHS = (D_BR, D_BR, D_BR, D_BR, N_HEADS * DK_B, N_HEADS * DK_B, D_BR, D_BR, GLA_RANK, D_MODEL, D_MODEL)

ROW_SUB_Q = 0
ROW_SUB_K = 1
ROW_QT = N_LEVELS
ROW_KH = N_LEVELS + 1
ROW_V = N_LEVELS + 2
N_OPERAND_ROWS = N_LEVELS + 3

OFF_QA = 0
OFF_FA = OFF_QA + D_BR
OFF_IA = OFF_FA + D_BR
OFF_GA = OFF_IA + D_BR
D_QK_B = N_HEADS * DK_B
OFF_QB = OFF_GA + D_BR
OFF_KB = OFF_QB + D_QK_B
OFF_VB = OFF_KB + D_QK_B
OFF_GB = OFF_VB + D_BR
OFF_AB = OFF_GB + D_BR
D_PROJ = OFF_AB + RANK_PAD
OFF_ZA = D_PROJ
OFF_ZB = OFF_ZA + D_MODEL
D_IN_PAD = OFF_ZB + D_MODEL

V7X_VMEM_LIMIT_BYTES = 57 * 1024 * 1024

_F32 = jnp.float32
_BF16 = jnp.bfloat16


def _dot(a, b):
    return jnp.dot(a, b, preferred_element_type=_F32)


def _dot_nt(a, b):
    return lax.dot_general(a, b, (((1,), (1,)), ((), ())), preferred_element_type=_F32)


def _dot_tn(a, b):
    return lax.dot_general(a, b, (((0,), (0,)), ((), ())), preferred_element_type=_F32)


def _rms(x, gain):
    n = x.shape[-1]
    ss = jnp.sum(x * x, axis=-1, keepdims=True)
    return x * lax.rsqrt(ss + n * NORM_EPS) * (gain * math.sqrt(n))


def _sigmoid(x):
    return 0.5 * jnp.tanh(0.5 * x) + 0.5


def _silu(x):
    hx = 0.5 * x
    return hx * jnp.tanh(hx) + hx


def _block_diag(a, b):
    zero = jnp.zeros_like(a)
    return jnp.concatenate([jnp.concatenate([a, zero], axis=1), jnp.concatenate([zero, b], axis=1)], axis=0)


def _spread_heads(x):
    lane = lax.broadcasted_iota(jnp.int32, (x.shape[0], D_HEAD), 1)
    per_tile = D_HEAD // DK_B
    tiles = []
    for h in range(N_HEADS):
        t = x[:, (h // per_tile) * D_HEAD:(h // per_tile + 1) * D_HEAD]
        shift = (h % per_tile) * DK_B
        if shift:
            t = pltpu.roll(t, D_HEAD - shift, axis=1)
        tiles.append(jnp.where(lane < DK_B, t, 0.0))
    return jnp.concatenate(tiles, axis=1)


def _level_constants():
    t = np.arange(CH)
    mats = []
    for l in range(N_MXU_LEVELS):
        s = 1 << l
        ref = (t & ~(2 * s - 1)) + s - 1
        odd = (t & s) != 0
        u = t[None, :]
        a = np.where(odd[:, None], (u > ref[:, None]) & (u <= t[:, None]),
                     (u > t[:, None]) & (u <= ref[:, None]))
        mats.append(a)
    mats.append(t[None, :] <= t[:, None])
    a_all = np.concatenate(mats, axis=0).astype(np.float32)
    a_cat = np.concatenate([a_all, a_all], axis=1)
    x = t[:, None] ^ t[None, :]
    lvl = np.where(x > 0, np.floor(np.log2(np.maximum(x, 1))).astype(np.int32), LVL_DIAG)
    lvl = np.where(t[:, None] >= t[None, :], lvl, LVL_NONE).astype(np.int32)
    return a_cat, np.concatenate([lvl, lvl], axis=1)


LVL_NONE = -1
LVL_DIAG = -2
MERGED_ROWS = BF16_ROWS
MERGED_LEVELS = MERGED_ROWS.bit_length() - 1
MERGED_MAX_EXP = 96.0
MERGED_MAX_KEY = 2.0 ** 24
EPILOGUE_PIECE = 512
GATE_PIECE = 512
N_PREP_STAGES = 1 + N_HEADS
N_PAIR_STAGES = N_HEADS // 2


def _gla_chunk(inputs_fn, v_fn, gate_fn, gain, amat_ref, lvl2, e_ref, qk_ref, w_ref, sc_ref, st_ref, o_ref, o_off,
               rows, merged):
    q, k, v, g2 = inputs_fn()
    qk_ref[0] = q
    qk_ref[1] = k
    w_ref[ROW_V] = v.astype(_BF16)
    g_hi = g2.astype(_BF16)
    g_lo = (g2 - g_hi.astype(_F32)).astype(_BF16)
    cum = slice(N_MXU_LEVELS * CH, (N_MXU_LEVELS + 1) * CH)
    need = cum if merged >= N_MXU_LEVELS else slice(0, cum.stop)
    e_ref[need] = _dot(amat_ref[need, :], jnp.concatenate([g_hi, g_lo], axis=0))
    yield

    sub = lax.broadcasted_iota(jnp.int32, (CH, D_HEAD), 0)
    odd_rows = [(sub & (1 << l)) != 0 for l in range(N_MXU_LEVELS)]
    d_last = []
    for ct in range(N_HEADS):
        cs = slice(ct * D_HEAD, (ct + 1) * D_HEAD)
        qc = qk_ref[0, :, cs]
        kc = qk_ref[1, :, cs]
        big_g = e_ref[cum, cs]
        g_last = big_g[CH - 1:CH, :]
        if merged:
            blk = 1 << merged
            n_blk = CH // blk
            g_prev = [big_g[b * blk - 1:b * blk, :] if b else jnp.zeros((1, D_HEAD), _F32) for b in range(n_blk)]
            q_blk, k_blk = [], []
            for b in range(n_blk):
                rel = big_g[b * blk:(b + 1) * blk] - g_prev[b]
                q_blk.append(qc[b * blk:(b + 1) * blk] * jnp.exp2(rel))
                k_blk.append(kc[b * blk:(b + 1) * blk] * jnp.exp2(-rel))

            def put(row, parts):
                w_ref[row, :, cs] = jnp.concatenate(parts, axis=0).astype(_BF16)

            put(ROW_SUB_Q, q_blk)
            put(ROW_SUB_K, k_blk)
            for l in range(merged, N_LEVELS):
                per = (1 << l) // blk
                parts = []
                for b in range(n_blk):
                    first = (b // per) * per
                    if (b // per) % 2:
                        parts.append(q_blk[b] * jnp.exp2(g_prev[b] - g_prev[first]))
                    else:
                        parts.append(k_blk[b] * jnp.exp2(g_prev[first + per] - g_prev[b]))
                put(l, parts)
            put(ROW_QT, [q_blk[b] * jnp.exp2(g_prev[b]) for b in range(n_blk)])
            put(ROW_KH, [k_blk[b] * jnp.exp2(g_last - g_prev[b]) for b in range(n_blk)])
            d_last.append(jnp.exp2(g_last))
            yield
            continue
        for l in range(N_LEVELS):
            s = 1 << l
            if l < N_MXU_LEVELS:
                w = jnp.where(odd_rows[l], qc, kc) * jnp.exp2(e_ref[l * CH:(l + 1) * CH, cs])
            else:
                parts = []
                for base in range(0, CH, 2 * s):
                    mid = base + s
                    g_ref_row = jnp.broadcast_to(big_g[mid - 1:mid, :], (s, D_HEAD))
                    parts.append(kc[base:mid] * jnp.exp2(g_ref_row - big_g[base:mid]))
                    parts.append(qc[mid:mid + s] * jnp.exp2(big_g[mid:mid + s] - g_ref_row))
                w = jnp.concatenate(parts, axis=0)
            w_ref[l, :, cs] = w.astype(_BF16)
        w_ref[ROW_QT, :, cs] = (qc * jnp.exp2(big_g)).astype(_BF16)
        w_ref[ROW_KH, :, cs] = (kc * jnp.exp2(g_last - big_g)).astype(_BF16)
        d_last.append(jnp.exp2(g_last))
        yield

    for pr in range(N_HEADS // 2):
        h0 = slice(pr * D_PAIR, pr * D_PAIR + D_HEAD)
        h1 = slice(pr * D_PAIR + D_HEAD, (pr + 1) * D_PAIR)
        ps = slice(pr * D_PAIR, (pr + 1) * D_PAIR)

        def keys(idx, lanes, s):
            if s < BF16_ROWS:
                return w_ref[idx, :, lanes]
            zero = jnp.zeros((s, D_HEAD), _BF16)
            return jnp.concatenate([zero if b % 2 else w_ref[idx, b * s:(b + 1) * s, lanes]
                                    for b in range(CH // s)], axis=0)

        low = jnp.zeros((CH, D_PAIR), _F32)
        if merged:
            r = _dot_nt(w_ref[ROW_SUB_Q, :, ps], _block_diag(w_ref[ROW_SUB_K, :, h0], w_ref[ROW_SUB_K, :, h1]))
            sc_ref[pr] = jnp.where((lvl2 != LVL_NONE) & (lvl2 < merged), r, low)
        for l in range(merged, N_LEVELS):
            s = 1 << l
            rhs = _block_diag(keys(l, h0, s), keys(l, h1, s))
            odd_blocks = [b for b in range(CH // s) if b % 2]
            if s >= BF16_ROWS:
                lhs = jnp.concatenate([w_ref[l, b * s:(b + 1) * s, ps] for b in odd_blocks], axis=0)
            else:
                lhs = w_ref[l, :, ps]
            r = _dot_nt(lhs, rhs)
            if s < SUBLANES:
                low = jnp.where(lvl2 == l, r, low)
                if 2 * s == SUBLANES:
                    sc_ref[pr] = low
                continue
            for i, b in enumerate(odd_blocks):
                src_row = i * s if s >= BF16_ROWS else b * s
                for head in range(2):
                    c0 = head * D_HEAD + (b - 1) * s
                    sc_ref[pr, b * s:(b + 1) * s, c0:c0 + s] = r[src_row:src_row + s, c0:c0 + s]
        sc = sc_ref[pr].astype(_BF16)

        st0 = st_ref[2 * pr]
        st1 = st_ref[2 * pr + 1]
        o = (_dot(sc, _block_diag(w_ref[ROW_V, :, h0], w_ref[ROW_V, :, h1]))
             + _dot_nt(w_ref[ROW_QT, :, ps], _block_diag(st0.astype(_BF16), st1.astype(_BF16))))
        st_ref[2 * pr] = st0 * d_last[2 * pr] + _dot_tn(w_ref[ROW_V, :, h0], w_ref[ROW_KH, :, h0])
        st_ref[2 * pr + 1] = st1 * d_last[2 * pr + 1] + _dot_tn(w_ref[ROW_V, :, h1], w_ref[ROW_KH, :, h1])
        for i, hs in enumerate((h0, h1)):
            o_h = o[:, i * D_HEAD:(i + 1) * D_HEAD]
            if not merged:
                self_score = jnp.sum(qk_ref[0, :, hs] * qk_ref[1, :, hs], axis=-1, keepdims=True)
                o_h = o_h + self_score * v_fn(hs)
            o_ref[rows, o_off + hs.start:o_off + hs.stop] = _rms(o_h, gain[:, hs]) * _silu(gate_fn(hs))
        yield


STAGE_ROWS = TL // 2


def _load_weights(jobs, zero_fills, stage_ref, sem):
    def copy(i):
        src, row, n, _, _ = jobs[i]
        slot = i % 2
        return pltpu.make_async_copy(src.at[pl.ds(row, n)], stage_ref.at[0, pl.ds(slot * STAGE_ROWS, n)],
                                     sem.at[slot])

    copy(0).start()
    for i, (_, _, n, dst, dst_row) in enumerate(jobs):
        if i + 1 < len(jobs):
            copy(i + 1).start()
        copy(i).wait()
        off = (i % 2) * STAGE_ROWS
        dst[dst_row:dst_row + n, :] = stage_ref[0, off:off + n, :].astype(_BF16)
    for dst, row, n in zero_fills:
        dst[row:row + n, :] = jnp.zeros((n, dst.shape[1]), _BF16)


def _weight_jobs(w_in_hbm, w_in_ref, others):
    src_off = np.cumsum((0,) + IN_WIDTHS)
    qa, fa, ia, ga, qb, kb, vb, gb, ab, za, zb = (int(o) for o in src_off[:-1])
    groups = [(OFF_QA, qa, D_BR), (OFF_FA, fa, D_BR), (OFF_IA, ia, D_BR), (OFF_GA, ga, D_BR),
              (OFF_QB, qb, D_QK_B), (OFF_KB, kb, D_QK_B), (OFF_VB, vb, D_BR), (OFF_GB, gb, D_BR),
              (OFF_AB, ab, GLA_RANK), (OFF_ZA, za, D_MODEL), (OFF_ZB, zb, D_MODEL)]
    jobs = []
    for dst_row, src_row, n in groups:
        for r in range(0, n, STAGE_ROWS):
            jobs.append((w_in_hbm, src_row + r, min(STAGE_ROWS, n - r), w_in_ref, dst_row + r))
    for hbm, ref in others:
        for r in range(0, ref.shape[0], STAGE_ROWS):
            jobs.append((hbm, r, min(STAGE_ROWS, ref.shape[0] - r), ref, r))
    return jobs, [(w_in_ref, OFF_AB + GLA_RANK, RANK_PAD - GLA_RANK)]


def _block_kernel(x_ref, p_ref, w_in_hbm, lb_ref, w_up_ref, b_gla_ref, n_pre_ref, n_post_ref,
                  hn_a_ref, hn_b_ref, w_ba_hbm, w_bb_hbm, w_out_hbm, w_ple_hbm, w_pg_hbm,
                  n_ple_ref, amat_ref, lvl_ref, out_ref,
                  proj_ref, gz_ref, e_ref, qka_ref, qkb_ref, wa_ref, wb_ref, sca_ref, scb_ref, st_a_ref, st_b_ref,
                  w_in_ref, w_ba_ref, w_bb_ref, w_out_ref, w_ple_ref, w_pg_ref, load_sem):
    @pl.when(jnp.logical_and(pl.program_id(0) == 0, pl.program_id(1) == 0))
    def _():
        jobs, zero_fills = _weight_jobs(w_in_hbm, w_in_ref, ((w_ba_hbm, w_ba_ref), (w_bb_hbm, w_bb_ref),
                                                           (w_out_hbm, w_out_ref), (w_ple_hbm, w_ple_ref),
                                                           (w_pg_hbm, w_pg_ref)))
        _load_weights(jobs, zero_fills, out_ref, load_sem)

    @pl.when(pl.program_id(1) == 0)
    def _():
        st_a_ref[...] = jnp.zeros_like(st_a_ref)
        st_b_ref[...] = jnp.zeros_like(st_b_ref)

    x = x_ref[0]
    h = _rms(x, n_pre_ref[...]).astype(_BF16)
    proj_ref[:, OFF_QB:D_PROJ] = _dot_nt(h, w_in_ref[OFF_QB:D_PROJ, :])

    def gate_pieces():
        def piece(a):
            def run():
                z = _dot_nt(h, w_in_ref[D_PROJ + a:D_PROJ + a + GATE_PIECE, :])
                gz_ref[:, a:a + GATE_PIECE] = _sigmoid(z).astype(_BF16)
            return run
        return [piece(a) for a in range(0, 2 * D_MODEL, GATE_PIECE)]

    logits = lb_ref[...]
    ex = jnp.exp(logits - jnp.max(logits, axis=0, keepdims=True))
    lb = ex[0:1, :] / jnp.sum(ex, axis=0, keepdims=True)
    fg_half = 0.5 * (1.0 - lb)
    fg_mid = lb + fg_half
    lvl2 = lvl_ref[...]

    step_exp = MERGED_MAX_EXP / MERGED_ROWS
    z_limit = step_exp * GLA_TAU / LOG2_E - math.log(2.0)
    a_max = jnp.max(jnp.abs(proj_ref[:, OFF_AB:OFF_AB + RANK_PAD].astype(_BF16).astype(_F32)))
    up_max = jnp.max(jnp.sum(jnp.abs(w_up_ref[...].astype(_F32)), axis=0, keepdims=True))
    z_bound = a_max * up_max + jnp.max(jnp.abs(b_gla_ref[...]))
    safe = jnp.logical_and(
        jnp.logical_and(jnp.min(lb) > 2.0 ** -step_exp, z_bound < z_limit),
        jnp.max(jnp.abs(proj_ref[:, OFF_KB:OFF_KB + D_QK_B])) < MERGED_MAX_KEY)

    def chunk_phase(merged):
        streams = []
        for c in range(TL // CH):
            rows = slice(c * CH, (c + 1) * CH)

            def col(off, width=D_BR, rows=rows):
                return proj_ref[rows, off:off + width]

            def inputs_a(col=col):
                forget = fg_mid + fg_half * jnp.tanh(0.5 * col(OFF_FA))
                return _silu(col(OFF_QA)) * (DK_A ** -0.5), 1.0 - forget, col(OFF_IA), jnp.log2(forget)

            def inputs_b(col=col):
                z = _dot(col(OFF_AB, RANK_PAD).astype(_BF16), w_up_ref[...]) + b_gla_ref[...]
                log2_alpha = (jnp.maximum(-z, 0.0) + jnp.log(1.0 + jnp.exp(-jnp.abs(z)))) * (-LOG2_E / GLA_TAU)
                q = _spread_heads(col(OFF_QB, D_QK_B)) * (DK_B ** -0.5)
                return q, _spread_heads(col(OFF_KB, D_QK_B)), col(OFF_VB), log2_alpha

            def lanes_of(off, col=col):
                return lambda hs: col(off + hs.start, hs.stop - hs.start)

            streams.append(_gla_chunk(inputs_a, lanes_of(OFF_IA), lanes_of(OFF_GA), hn_a_ref[...], amat_ref, lvl2,
                                      e_ref, qka_ref, wa_ref, sca_ref, st_a_ref, proj_ref, OFF_QA, rows, merged))
            streams.append(_gla_chunk(inputs_b, lanes_of(OFF_VB), lanes_of(OFF_GB), hn_b_ref[...], amat_ref, lvl2,
                                      e_ref, qkb_ref, wb_ref, scb_ref, st_b_ref, proj_ref, OFF_QB, rows, merged))

        def advance(stream, n):
            for _ in range(n):
                next(stream, None)

        share = [N_PREP_STAGES // N_PAIR_STAGES + (j < N_PREP_STAGES % N_PAIR_STAGES)
                 for j in range(N_PAIR_STAGES)]
        gates = gate_pieces()
        for j in range(max(N_PREP_STAGES, len(gates))):
            if j < len(gates):
                gates[j]()
            if j < N_PREP_STAGES:
                advance(streams[0], 1)
        for i, cur in enumerate(streams):
            nxt = streams[i + 1] if i + 1 < len(streams) else None
            for j in range(N_PAIR_STAGES):
                if nxt is not None:
                    advance(nxt, 1)
                advance(cur, 1)
                if nxt is not None:
                    advance(nxt, share[j] - 1)

    proj_ref[:, 0:OFF_QB] = _dot_nt(h, w_in_ref[0:OFF_QB, :])

    pl.when(safe)(lambda: chunk_phase(MERGED_LEVELS))
    pl.when(jnp.logical_not(safe))(lambda: chunk_phase(0))

    ya = _dot(proj_ref[:, OFF_QA:OFF_QA + D_BR].astype(_BF16), w_ba_ref[...])
    yb = _dot(proj_ref[:, OFF_QB:OFF_QB + D_BR].astype(_BF16), w_bb_ref[...])
    y = gz_ref[:, 0:D_MODEL] * ya + gz_ref[:, D_MODEL:2 * D_MODEL] * yb
    y = _dot(y.astype(_BF16), w_out_ref[...])
    x1 = x + _rms(y, n_post_ref[...])
    e = _rms(_dot(p_ref[0, 0].astype(_BF16), w_ple_ref[...]), n_ple_ref[...])
    x1b = x1.astype(_BF16)
    for c in range(0, D_MODEL, EPILOGUE_PIECE):
        cols = slice(c, c + EPILOGUE_PIECE)
        out_ref[0, :, cols] = x1[:, cols] + _sigmoid(_dot(x1b, w_pg_ref[:, cols])) * e[:, cols]


def _pad_heads(w, dk):
    lead = w.shape[:-1]
    w = w.reshape(lead + (N_HEADS, dk))
    w = jnp.pad(w, [(0, 0)] * len(lead) + [(0, 0), (0, D_HEAD - dk)])
    return w.reshape(lead + (N_HEADS * D_HEAD,))


def _const_spec(shape):
    nd = len(shape)
    return pl.BlockSpec(shape, lambda b, t: (0,) * nd, pipeline_mode=pl.Buffered(1))


def kernel(x, p, w_in, lb_logits, w_gla_up, b_gla, norm_pre, norm_post, head_norm_a, head_norm_b,
           w_branch_a, w_branch_b, w_out, w_ple, w_ple_gate, norm_ple):
    bsz, seq, _ = x.shape
    assert x.shape == (bsz, seq, D_MODEL) and seq % TL == 0 and w_in.shape[0] == 1 and p.shape[0] == 1

    assert w_in.shape == (1, D_MODEL, sum(IN_WIDTHS))
    weights = (w_in[0].T, w_branch_a[0], w_branch_b[0], w_out[0], w_ple[0], w_ple_gate[0])
    for w in weights:
        assert w.shape[1] == D_MODEL and w.shape[0] % BF16_ROWS == 0
    w_up = jnp.pad(_pad_heads(w_gla_up[0], DK_B), ((0, RANK_PAD - GLA_RANK), (0, 0))).astype(_BF16)
    b_up = _pad_heads(b_gla[0], DK_B)[None, :]

    a_cat, lvl2 = _level_constants()
    amat = jnp.asarray(a_cat, dtype=_BF16)
    lvl2 = jnp.asarray(lvl2)

    row = lambda a: a[0][None, :]
    operands = (
        x, p, weights[0], lb_logits, w_up, b_up, row(norm_pre), row(norm_post), row(head_norm_a), row(head_norm_b),
        *weights[1:], row(norm_ple), amat, lvl2,
    )
    in_hbm = {2, 10, 11, 12, 13, 14}
    in_specs = [
        pl.BlockSpec((1, TL, D_MODEL), lambda b, t: (b, t, 0)),
        pl.BlockSpec((1, 1, TL, D_PLE), lambda b, t: (0, b, t, 0)),
    ] + [pl.BlockSpec(memory_space=pl.ANY) if i in in_hbm else _const_spec(a.shape)
         for i, a in enumerate(operands) if i >= 2]

    return pl.pallas_call(
        _block_kernel,
        grid=(bsz, seq // TL),
        in_specs=in_specs,
        out_specs=pl.BlockSpec((1, TL, D_MODEL), lambda b, t: (b, t, 0)),
        out_shape=jax.ShapeDtypeStruct(x.shape, x.dtype),
        scratch_shapes=[
            pltpu.VMEM((TL, D_PROJ), _F32),
            pltpu.VMEM((TL, 2 * D_MODEL), _BF16),
            pltpu.VMEM(((N_MXU_LEVELS + 1) * CH, D_BR), _F32),
            pltpu.VMEM((2, CH, D_BR), _F32),
            pltpu.VMEM((2, CH, D_BR), _F32),
            pltpu.VMEM((N_OPERAND_ROWS, CH, D_BR), _BF16),
            pltpu.VMEM((N_OPERAND_ROWS, CH, D_BR), _BF16),
            pltpu.VMEM((N_HEADS // 2, CH, D_PAIR), _F32),
            pltpu.VMEM((N_HEADS // 2, CH, D_PAIR), _F32),
            pltpu.VMEM((N_HEADS, D_HEAD, D_HEAD), _F32),
            pltpu.VMEM((N_HEADS, D_HEAD, D_HEAD), _F32),
            pltpu.VMEM((D_IN_PAD, D_MODEL), _BF16),
            pltpu.VMEM((D_BR, D_MODEL), _BF16),
            pltpu.VMEM((D_BR, D_MODEL), _BF16),
            pltpu.VMEM((D_MODEL, D_MODEL), _BF16),
            pltpu.VMEM((D_PLE, D_MODEL), _BF16),
            pltpu.VMEM((D_MODEL, D_MODEL), _BF16),
            pltpu.SemaphoreType.DMA((2,)),
        ],
        compiler_params=pltpu.CompilerParams(
            dimension_semantics=("arbitrary", "arbitrary"),
            vmem_limit_bytes=V7X_VMEM_LIMIT_BYTES,
        ),
        name="hgrn2_gla_block",
    )(*operands)
```

```python
import math

import numpy as np
import jax
import jax.numpy as jnp
from jax import lax
from jax.experimental import pallas as pl
from jax.experimental.pallas import tpu as pltpu

D_MODEL = 1024
D_PLE = 256
N_HEADS = 4
D_HEAD = 128
D_BR = N_HEADS * D_HEAD
D_PAIR = 2 * D_HEAD
DK_A = 128
DK_B = 64
GLA_RANK = 16
GLA_TAU = 16.0
NORM_EPS = 1e-6
LOG2_E = 1.4426950408889634

SUBLANES = 8
BF16_ROWS = 16
N_MXU_LEVELS = SUBLANES.bit_length() - 1

CH = 128
N_LEVELS = CH.bit_length() - 1
TL = 512
RANK_PAD = 128
IN_WIDTHS = (D_BR, D_BR, D_BR, D_BR, N_HEADS * DK_B, N_HEADS * DK_B, D_BR, D_BR, GLA_RANK, D_MODEL, D_MODEL)

ROW_SUB_Q = 0
ROW_SUB_K = 1
ROW_QT = N_LEVELS
ROW_KH = N_LEVELS + 1
ROW_V = N_LEVELS + 2
N_OPERAND_ROWS = N_LEVELS + 3

OFF_QA = 0
OFF_FA = OFF_QA + D_BR
OFF_IA = OFF_FA + D_BR
OFF_GA = OFF_IA + D_BR
D_QK_B = N_HEADS * DK_B
OFF_QB = OFF_GA + D_BR
OFF_KB = OFF_QB + D_QK_B
OFF_VB = OFF_KB + D_QK_B
OFF_GB = OFF_VB + D_BR
OFF_AB = OFF_GB + D_BR
D_PROJ = OFF_AB + RANK_PAD
OFF_ZA = D_PROJ
OFF_ZB = OFF_ZA + D_MODEL
D_IN_PAD = OFF_ZB + D_MODEL

V7X_VMEM_LIMIT_BYTES = 57 * 1024 * 1024

_F32 = jnp.float32
_BF16 = jnp.bfloat16


def _dot(a, b):
    return jnp.dot(a, b, preferred_element_type=_F32)


def _dot_nt(a, b):
    return lax.dot_general(a, b, (((1,), (1,)), ((), ())), preferred_element_type=_F32)


def _dot_tn(a, b):
    return lax.dot_general(a, b, (((0,), (0,)), ((), ())), preferred_element_type=_F32)


def _rms(x, gain):
    n = x.shape[-1]
    ss = jnp.sum(x * x, axis=-1, keepdims=True)
    return x * lax.rsqrt(ss + n * NORM_EPS) * (gain * math.sqrt(n))


def _sigmoid(x):
    return 0.5 * jnp.tanh(0.5 * x) + 0.5


def _silu(x):
    hx = 0.5 * x
    return hx * jnp.tanh(hx) + hx


def _block_diag(a, b):
    zero = jnp.zeros_like(a)
    return jnp.concatenate([jnp.concatenate([a, zero], axis=1), jnp.concatenate([zero, b], axis=1)], axis=0)


def _spread_heads(x):
    lane = lax.broadcasted_iota(jnp.int32, (x.shape[0], D_HEAD), 1)
    per_tile = D_HEAD // DK_B
    tiles = []
    for h in range(N_HEADS):
        t = x[:, (h // per_tile) * D_HEAD:(h // per_tile + 1) * D_HEAD]
        shift = (h % per_tile) * DK_B
        if shift:
            t = pltpu.roll(t, D_HEAD - shift, axis=1)
        tiles.append(jnp.where(lane < DK_B, t, 0.0))
    return jnp.concatenate(tiles, axis=1)


def _level_constants():
    t = np.arange(CH)
    mats = []
    for l in range(N_MXU_LEVELS):
        s = 1 << l
        ref = (t & ~(2 * s - 1)) + s - 1
        odd = (t & s) != 0
        u = t[None, :]
        a = np.where(odd[:, None], (u > ref[:, None]) & (u <= t[:, None]),
                     (u > t[:, None]) & (u <= ref[:, None]))
        mats.append(a)
    mats.append(t[None, :] <= t[:, None])
    a_all = np.concatenate(mats, axis=0).astype(np.float32)
    a_cat = np.concatenate([a_all, a_all], axis=1)
    x = t[:, None] ^ t[None, :]
    lvl = np.where(x > 0, np.floor(np.log2(np.maximum(x, 1))).astype(np.int32), LVL_DIAG)
    lvl = np.where(t[:, None] >= t[None, :], lvl, LVL_NONE).astype(np.int32)
    return a_cat, np.concatenate([lvl, lvl], axis=1)


LVL_NONE = -1
LVL_DIAG = -2
MERGED_ROWS = BF16_ROWS
MERGED_LEVELS = MERGED_ROWS.bit_length() - 1
MERGED_MAX_EXP = 96.0
MERGED_MAX_KEY = 2.0 ** 24
EPILOGUE_PIECE = 512
GATE_PIECE = 512
N_PREP_STAGES = 1 + N_HEADS
N_PAIR_STAGES = N_HEADS // 2


def _gla_chunk(inputs_fn, v_fn, gate_fn, gain, amat_ref, lvl2, e_ref, qk_ref, w_ref, sc_ref, st_ref, o_ref, o_off,
               rows, merged):
    q, k, v, g2 = inputs_fn()
    qk_ref[0] = q
    qk_ref[1] = k
    w_ref[ROW_V] = v.astype(_BF16)
    g_hi = g2.astype(_BF16)
    g_lo = (g2 - g_hi.astype(_F32)).astype(_BF16)
    cum = slice(N_MXU_LEVELS * CH, (N_MXU_LEVELS + 1) * CH)
    need = cum if merged >= N_MXU_LEVELS else slice(0, cum.stop)
    e_ref[need] = _dot(amat_ref[need, :], jnp.concatenate([g_hi, g_lo], axis=0))
    yield

    sub = lax.broadcasted_iota(jnp.int32, (CH, D_HEAD), 0)
    odd_rows = [(sub & (1 << l)) != 0 for l in range(N_MXU_LEVELS)]
    d_last = []
    for ct in range(N_HEADS):
        cs = slice(ct * D_HEAD, (ct + 1) * D_HEAD)
        qc = qk_ref[0, :, cs]
        kc = qk_ref[1, :, cs]
        big_g = e_ref[cum, cs]
        g_last = big_g[CH - 1:CH, :]
        if merged:
            blk = 1 << merged
            n_blk = CH // blk
            g_prev = [big_g[b * blk - 1:b * blk, :] if b else jnp.zeros((1, D_HEAD), _F32) for b in range(n_blk)]
            q_blk, k_blk = [], []
            for b in range(n_blk):
                rel = big_g[b * blk:(b + 1) * blk] - g_prev[b]
                q_blk.append(qc[b * blk:(b + 1) * blk] * jnp.exp2(rel))
                k_blk.append(kc[b * blk:(b + 1) * blk] * jnp.exp2(-rel))

            def put(row, parts):
                w_ref[row, :, cs] = jnp.concatenate(parts, axis=0).astype(_BF16)

            put(ROW_SUB_Q, q_blk)
            put(ROW_SUB_K, k_blk)
            for l in range(merged, N_LEVELS):
                per = (1 << l) // blk
                parts = []
                for b in range(n_blk):
                    first = (b // per) * per
                    if (b // per) % 2:
                        parts.append(q_blk[b] * jnp.exp2(g_prev[b] - g_prev[first]))
                    else:
                        parts.append(k_blk[b] * jnp.exp2(g_prev[first + per] - g_prev[b]))
                put(l, parts)
            put(ROW_QT, [q_blk[b] * jnp.exp2(g_prev[b]) for b in range(n_blk)])
            put(ROW_KH, [k_blk[b] * jnp.exp2(g_last - g_prev[b]) for b in range(n_blk)])
            d_last.append(jnp.exp2(g_last))
            yield
            continue
        for l in range(N_LEVELS):
            s = 1 << l
            if l < N_MXU_LEVELS:
                w = jnp.where(odd_rows[l], qc, kc) * jnp.exp2(e_ref[l * CH:(l + 1) * CH, cs])
            else:
                parts = []
                for base in range(0, CH, 2 * s):
                    mid = base + s
                    g_ref_row = jnp.broadcast_to(big_g[mid - 1:mid, :], (s, D_HEAD))
                    parts.append(kc[base:mid] * jnp.exp2(g_ref_row - big_g[base:mid]))
                    parts.append(qc[mid:mid + s] * jnp.exp2(big_g[mid:mid + s] - g_ref_row))
                w = jnp.concatenate(parts, axis=0)
            w_ref[l, :, cs] = w.astype(_BF16)
        w_ref[ROW_QT, :, cs] = (qc * jnp.exp2(big_g)).astype(_BF16)
        w_ref[ROW_KH, :, cs] = (kc * jnp.exp2(g_last - big_g)).astype(_BF16)
        d_last.append(jnp.exp2(g_last))
        yield

    for pr in range(N_HEADS // 2):
        h0 = slice(pr * D_PAIR, pr * D_PAIR + D_HEAD)
        h1 = slice(pr * D_PAIR + D_HEAD, (pr + 1) * D_PAIR)
        ps = slice(pr * D_PAIR, (pr + 1) * D_PAIR)

        def keys(idx, lanes, s):
            if s < BF16_ROWS:
                return w_ref[idx, :, lanes]
            zero = jnp.zeros((s, D_HEAD), _BF16)
            return jnp.concatenate([zero if b % 2 else w_ref[idx, b * s:(b + 1) * s, lanes]
                                    for b in range(CH // s)], axis=0)

        low = jnp.zeros((CH, D_PAIR), _F32)
        if merged:
            r = _dot_nt(w_ref[ROW_SUB_Q, :, ps], _block_diag(w_ref[ROW_SUB_K, :, h0], w_ref[ROW_SUB_K, :, h1]))
            sc_ref[pr] = jnp.where((lvl2 != LVL_NONE) & (lvl2 < merged), r, low)
        for l in range(merged, N_LEVELS):
            s = 1 << l
            rhs = _block_diag(keys(l, h0, s), keys(l, h1, s))
            odd_blocks = [b for b in range(CH // s) if b % 2]
            if s >= BF16_ROWS:
                lhs = jnp.concatenate([w_ref[l, b * s:(b + 1) * s, ps] for b in odd_blocks], axis=0)
            else:
                lhs = w_ref[l, :, ps]
            r = _dot_nt(lhs, rhs)
            if s < SUBLANES:
                low = jnp.where(lvl2 == l, r, low)
                if 2 * s == SUBLANES:
                    sc_ref[pr] = low
                continue
            for i, b in enumerate(odd_blocks):
                src_row = i * s if s >= BF16_ROWS else b * s
                for head in range(2):
                    c0 = head * D_HEAD + (b - 1) * s
                    sc_ref[pr, b * s:(b + 1) * s, c0:c0 + s] = r[src_row:src_row + s, c0:c0 + s]
        sc = sc_ref[pr].astype(_BF16)

        st0 = st_ref[2 * pr]
        st1 = st_ref[2 * pr + 1]
        o = (_dot(sc, _block_diag(w_ref[ROW_V, :, h0], w_ref[ROW_V, :, h1]))
             + _dot_nt(w_ref[ROW_QT, :, ps], _block_diag(st0.astype(_BF16), st1.astype(_BF16))))
        st_ref[2 * pr] = st0 * d_last[2 * pr] + _dot_tn(w_ref[ROW_V, :, h0], w_ref[ROW_KH, :, h0])
        st_ref[2 * pr + 1] = st1 * d_last[2 * pr + 1] + _dot_tn(w_ref[ROW_V, :, h1], w_ref[ROW_KH, :, h1])
        for i, hs in enumerate((h0, h1)):
            o_h = o[:, i * D_HEAD:(i + 1) * D_HEAD]
            if not merged:
                self_score = jnp.sum(qk_ref[0, :, hs] * qk_ref[1, :, hs], axis=-1, keepdims=True)
                o_h = o_h + self_score * v_fn(hs)
            o_ref[rows, o_off + hs.start:o_off + hs.stop] = _rms(o_h, gain[:, hs]) * _silu(gate_fn(hs))
        yield


STAGE_ROWS = TL // 2
N_LOAD_STAGES = 8


def _load_weights(jobs, zero_fills, stages, sem):
    depth = len(stages)

    def copy(i):
        src, row, n, _, _ = jobs[i]
        return pltpu.make_async_copy(src.at[pl.ds(row, n)], stages[i % depth](n), sem.at[i % depth])

    for i in range(min(depth - 1, len(jobs))):
        copy(i).start()
    for i, (_, _, n, dst, dst_row) in enumerate(jobs):
        if i + depth - 1 < len(jobs):
            copy(i + depth - 1).start()
        copy(i).wait()
        dst[dst_row:dst_row + n, :] = stages[i % depth](n)[...].astype(_BF16)
    for dst, row, n in zero_fills:
        dst[row:row + n, :] = jnp.zeros((n, dst.shape[1]), _BF16)


def _weight_jobs(w_in_hbm, w_in_ref, others):
    src_off = np.cumsum((0,) + IN_WIDTHS)
    qa, fa, ia, ga, qb, kb, vb, gb, ab, za, zb = (int(o) for o in src_off[:-1])
    groups = [(OFF_QA, qa, D_BR), (OFF_FA, fa, D_BR), (OFF_IA, ia, D_BR), (OFF_GA, ga, D_BR),
              (OFF_QB, qb, D_QK_B), (OFF_KB, kb, D_QK_B), (OFF_VB, vb, D_BR), (OFF_GB, gb, D_BR),
              (OFF_AB, ab, GLA_RANK), (OFF_ZA, za, D_MODEL), (OFF_ZB, zb, D_MODEL)]
    jobs = []
    for dst_row, src_row, n in groups:
        for r in range(0, n, STAGE_ROWS):
            jobs.append((w_in_hbm, src_row + r, min(STAGE_ROWS, n - r), w_in_ref, dst_row + r))
    for hbm, ref in others:
        for r in range(0, ref.shape[0], STAGE_ROWS):
            jobs.append((hbm, r, min(STAGE_ROWS, ref.shape[0] - r), ref, r))
    return jobs, [(w_in_ref, OFF_AB + GLA_RANK, RANK_PAD - GLA_RANK)]


def _block_kernel(x_ref, p_ref, w_in_hbm, lb_ref, w_up_ref, b_gla_ref, n_pre_ref, n_post_ref,
                  hn_a_ref, hn_b_ref, w_ba_hbm, w_bb_hbm, w_out_hbm, w_ple_hbm, w_pg_hbm,
                  n_ple_ref, amat_ref, lvl_ref, out_ref,
                  proj_ref, gz_ref, e_ref, qka_ref, qkb_ref, wa_ref, wb_ref, sca_ref, scb_ref, st_a_ref, st_b_ref,
                  w_in_ref, w_ba_ref, w_bb_ref, w_out_ref, w_ple_ref, w_pg_ref, load_sem):
    @pl.when(jnp.logical_and(pl.program_id(0) == 0, pl.program_id(1) == 0))
    def _():
        jobs, zero_fills = _weight_jobs(w_in_hbm, w_in_ref, ((w_ba_hbm, w_ba_ref), (w_bb_hbm, w_bb_ref),
                                                           (w_out_hbm, w_out_ref), (w_ple_hbm, w_ple_ref),
                                                           (w_pg_hbm, w_pg_ref)))
        stages = [lambda n, r=r: out_ref.at[0, pl.ds(r, n), :] for r in range(0, TL, STAGE_ROWS)]
        stages += [lambda n, r=r, c=c: proj_ref.at[pl.ds(r, n), pl.ds(c, D_MODEL)]
                   for c in range(0, D_PROJ - D_MODEL + 1, D_MODEL) for r in range(0, TL, STAGE_ROWS)]
        _load_weights(jobs, zero_fills, stages[:N_LOAD_STAGES], load_sem)

    @pl.when(pl.program_id(1) == 0)
    def _():
        st_a_ref[...] = jnp.zeros_like(st_a_ref)
        st_b_ref[...] = jnp.zeros_like(st_b_ref)

    x = x_ref[0]
    h = _rms(x, n_pre_ref[...]).astype(_BF16)
    proj_ref[:, OFF_QB:D_PROJ] = _dot_nt(h, w_in_ref[OFF_QB:D_PROJ, :])

    def gate_pieces():
        def piece(a):
            def run():
                z = _dot_nt(h, w_in_ref[D_PROJ + a:D_PROJ + a + GATE_PIECE, :])
                gz_ref[:, a:a + GATE_PIECE] = _sigmoid(z).astype(_BF16)
            return run
        return [piece(a) for a in range(0, 2 * D_MODEL, GATE_PIECE)]

    logits = lb_ref[...]
    ex = jnp.exp(logits - jnp.max(logits, axis=0, keepdims=True))
    lb = ex[0:1, :] / jnp.sum(ex, axis=0, keepdims=True)
    fg_half = 0.5 * (1.0 - lb)
    fg_mid = lb + fg_half
    lvl2 = lvl_ref[...]

    step_exp = MERGED_MAX_EXP / MERGED_ROWS
    z_limit = step_exp * GLA_TAU / LOG2_E - math.log(2.0)
    a_max = jnp.max(jnp.abs(proj_ref[:, OFF_AB:OFF_AB + RANK_PAD].astype(_BF16).astype(_F32)))
    up_max = jnp.max(jnp.sum(jnp.abs(w_up_ref[...].astype(_F32)), axis=0, keepdims=True))
    z_bound = a_max * up_max + jnp.max(jnp.abs(b_gla_ref[...]))
    safe = jnp.logical_and(
        jnp.logical_and(jnp.min(lb) > 2.0 ** -step_exp, z_bound < z_limit),
        jnp.max(jnp.abs(proj_ref[:, OFF_KB:OFF_KB + D_QK_B])) < MERGED_MAX_KEY)

    def chunk_phase(merged):
        streams = []
        for c in range(TL // CH):
            rows = slice(c * CH, (c + 1) * CH)

            def col(off, width=D_BR, rows=rows):
                return proj_ref[rows, off:off + width]

            def inputs_a(col=col):
                forget = fg_mid + fg_half * jnp.tanh(0.5 * col(OFF_FA))
                return _silu(col(OFF_QA)) * (DK_A ** -0.5), 1.0 - forget, col(OFF_IA), jnp.log2(forget)

            def inputs_b(col=col):
                z = _dot(col(OFF_AB, RANK_PAD).astype(_BF16), w_up_ref[...]) + b_gla_ref[...]
                log2_alpha = (jnp.maximum(-z, 0.0) + jnp.log(1.0 + jnp.exp(-jnp.abs(z)))) * (-LOG2_E / GLA_TAU)
                q = _spread_heads(col(OFF_QB, D_QK_B)) * (DK_B ** -0.5)
                return q, _spread_heads(col(OFF_KB, D_QK_B)), col(OFF_VB), log2_alpha

            def lanes_of(off, col=col):
                return lambda hs: col(off + hs.start, hs.stop - hs.start)

            streams.append(_gla_chunk(inputs_a, lanes_of(OFF_IA), lanes_of(OFF_GA), hn_a_ref[...], amat_ref, lvl2,
                                      e_ref, qka_ref, wa_ref, sca_ref, st_a_ref, proj_ref, OFF_QA, rows, merged))
            streams.append(_gla_chunk(inputs_b, lanes_of(OFF_VB), lanes_of(OFF_GB), hn_b_ref[...], amat_ref, lvl2,
                                      e_ref, qkb_ref, wb_ref, scb_ref, st_b_ref, proj_ref, OFF_QB, rows, merged))

        def advance(stream, n):
            for _ in range(n):
                next(stream, None)

        share = [N_PREP_STAGES // N_PAIR_STAGES + (j < N_PREP_STAGES % N_PAIR_STAGES)
                 for j in range(N_PAIR_STAGES)]
        gates = gate_pieces()
        for j in range(max(N_PREP_STAGES, len(gates))):
            if j < len(gates):
                gates[j]()
            if j < N_PREP_STAGES:
                advance(streams[0], 1)
        for i, cur in enumerate(streams):
            nxt = streams[i + 1] if i + 1 < len(streams) else None
            for j in range(N_PAIR_STAGES):
                if nxt is not None:
                    advance(nxt, 1)
                advance(cur, 1)
                if nxt is not None:
                    advance(nxt, share[j] - 1)

    proj_ref[:, 0:OFF_QB] = _dot_nt(h, w_in_ref[0:OFF_QB, :])

    pl.when(safe)(lambda: chunk_phase(MERGED_LEVELS))
    pl.when(jnp.logical_not(safe))(lambda: chunk_phase(0))

    ya = _dot(proj_ref[:, OFF_QA:OFF_QA + D_BR].astype(_BF16), w_ba_ref[...])
    yb = _dot(proj_ref[:, OFF_QB:OFF_QB + D_BR].astype(_BF16), w_bb_ref[...])
    y = gz_ref[:, 0:D_MODEL] * ya + gz_ref[:, D_MODEL:2 * D_MODEL] * yb
    y = _dot(y.astype(_BF16), w_out_ref[...])
    x1 = x + _rms(y, n_post_ref[...])
    e = _rms(_dot(p_ref[0, 0].astype(_BF16), w_ple_ref[...]), n_ple_ref[...])
    x1b = x1.astype(_BF16)
    for c in range(0, D_MODEL, EPILOGUE_PIECE):
        cols = slice(c, c + EPILOGUE_PIECE)
        out_ref[0, :, cols] = x1[:, cols] + _sigmoid(_dot(x1b, w_pg_ref[:, cols])) * e[:, cols]


def _pad_heads(w, dk):
    lead = w.shape[:-1]
    w = w.reshape(lead + (N_HEADS, dk))
    w = jnp.pad(w, [(0, 0)] * len(lead) + [(0, 0), (0, D_HEAD - dk)])
    return w.reshape(lead + (N_HEADS * D_HEAD,))


def _const_spec(shape):
    nd = len(shape)
    return pl.BlockSpec(shape, lambda b, t: (0,) * nd, pipeline_mode=pl.Buffered(1))


def kernel(x, p, w_in, lb_logits, w_gla_up, b_gla, norm_pre, norm_post, head_norm_a, head_norm_b,
           w_branch_a, w_branch_b, w_out, w_ple, w_ple_gate, norm_ple):
    bsz, seq, _ = x.shape
    assert x.shape == (bsz, seq, D_MODEL) and seq % TL == 0 and w_in.shape[0] == 1 and p.shape[0] == 1

    assert w_in.shape == (1, D_MODEL, sum(IN_WIDTHS))
    weights = (w_in[0].T, w_branch_a[0], w_branch_b[0], w_out[0], w_ple[0], w_ple_gate[0])
    for w in weights:
        assert w.shape[1] == D_MODEL and w.shape[0] % BF16_ROWS == 0
    w_up = jnp.pad(_pad_heads(w_gla_up[0], DK_B), ((0, RANK_PAD - GLA_RANK), (0, 0))).astype(_BF16)
    b_up = _pad_heads(b_gla[0], DK_B)[None, :]

    a_cat, lvl2 = _level_constants()
    amat = jnp.asarray(a_cat, dtype=_BF16)
    lvl2 = jnp.asarray(lvl2)

    row = lambda a: a[0][None, :]
    operands = (
        x, p, weights[0], lb_logits, w_up, b_up, row(norm_pre), row(norm_post), row(head_norm_a), row(head_norm_b),
        *weights[1:], row(norm_ple), amat, lvl2,
    )
    in_hbm = {2, 10, 11, 12, 13, 14}
    in_specs = [
        pl.BlockSpec((1, TL, D_MODEL), lambda b, t: (b, t, 0)),
        pl.BlockSpec((1, 1, TL, D_PLE), lambda b, t: (0, b, t, 0)),
    ] + [pl.BlockSpec(memory_space=pl.ANY) if i in in_hbm else _const_spec(a.shape)
         for i, a in enumerate(operands) if i >= 2]

    return pl.pallas_call(
        _block_kernel,
        grid=(bsz, seq // TL),
        in_specs=in_specs,
        out_specs=pl.BlockSpec((1, TL, D_MODEL), lambda b, t: (b, t, 0)),
        out_shape=jax.ShapeDtypeStruct(x.shape, x.dtype),
        scratch_shapes=[
            pltpu.VMEM((TL, D_PROJ), _F32),
            pltpu.VMEM((TL, 2 * D_MODEL), _BF16),
            pltpu.VMEM(((N_MXU_LEVELS + 1) * CH, D_BR), _F32),
            pltpu.VMEM((2, CH, D_BR), _F32),
            pltpu.VMEM((2, CH, D_BR), _F32),
            pltpu.VMEM((N_OPERAND_ROWS, CH, D_BR), _BF16),
            pltpu.VMEM((N_OPERAND_ROWS, CH, D_BR), _BF16),
            pltpu.VMEM((N_HEADS // 2, CH, D_PAIR), _F32),
            pltpu.VMEM((N_HEADS // 2, CH, D_PAIR), _F32),
            pltpu.VMEM((N_HEADS, D_HEAD, D_HEAD), _F32),
            pltpu.VMEM((N_HEADS, D_HEAD, D_HEAD), _F32),
            pltpu.VMEM((D_IN_PAD, D_MODEL), _BF16),
            pltpu.VMEM((D_BR, D_MODEL), _BF16),
            pltpu.VMEM((D_BR, D_MODEL), _BF16),
            pltpu.VMEM((D_MODEL, D_MODEL), _BF16),
            pltpu.VMEM((D_PLE, D_MODEL), _BF16),
            pltpu.VMEM((D_MODEL, D_MODEL), _BF16),
            pltpu.SemaphoreType.DMA((N_LOAD_STAGES,)),
        ],
        compiler_params=pltpu.CompilerParams(
            dimension_semantics=("arbitrary", "arbitrary"),
            vmem_limit_bytes=V7X_VMEM_LIMIT_BYTES,
        ),
        name="hgrn2_gla_block",
    )(*operands)
```

```python
import math

import numpy as np
import jax
import jax.numpy as jnp
from jax import lax
from jax.experimental import pallas as pl
from jax.experimental.pallas import tpu as pltpu

D_MODEL = 1024
D_PLE = 256
N_HEADS = 4
D_HEAD = 128
D_BR = N_HEADS * D_HEAD
D_PAIR = 2 * D_HEAD
DK_A = 128
DK_B = 64
GLA_RANK = 16
GLA_TAU = 16.0
NORM_EPS = 1e-6
LOG2_E = 1.4426950408889634

SUBLANES = 8
BF16_ROWS = 16
N_MXU_LEVELS = SUBLANES.bit_length() - 1

CH = 128
N_LEVELS = CH.bit_length() - 1
TL = 512
RANK_PAD = 128
IN_WIDTHS = (D_BR, D_BR, D_BR, D_BR, N_HEADS * DK_B, N_HEADS * DK_B, D_BR, D_BR, GLA_RANK, D_MODEL, D_MODEL)

ROW_SUB_Q = 0
ROW_SUB_K = 1
ROW_QT = N_LEVELS
ROW_KH = N_LEVELS + 1
ROW_V = N_LEVELS + 2
N_OPERAND_ROWS = N_LEVELS + 3

OFF_QA = 0
OFF_FA = OFF_QA + D_BR
OFF_IA = OFF_FA + D_BR
OFF_GA = OFF_IA + D_BR
D_QK_B = N_HEADS * DK_B
OFF_QB = OFF_GA + D_BR
OFF_KB = OFF_QB + D_QK_B
OFF_VB = OFF_KB + D_QK_B
OFF_GB = OFF_VB + D_BR
OFF_AB = OFF_GB + D_BR
D_PROJ = OFF_AB + RANK_PAD
OFF_ZA = D_PROJ
OFF_ZB = OFF_ZA + D_MODEL
D_IN_PAD = OFF_ZB + D_MODEL

V7X_VMEM_LIMIT_BYTES = 57 * 1024 * 1024

_F32 = jnp.float32
_BF16 = jnp.bfloat16


def _dot(a, b):
    return jnp.dot(a, b, preferred_element_type=_F32)


def _dot_nt(a, b):
    return lax.dot_general(a, b, (((1,), (1,)), ((), ())), preferred_element_type=_F32)


def _dot_tn(a, b):
    return lax.dot_general(a, b, (((0,), (0,)), ((), ())), preferred_element_type=_F32)


def _rms(x, gain):
    n = x.shape[-1]
    ss = jnp.sum(x * x, axis=-1, keepdims=True)
    return x * lax.rsqrt(ss + n * NORM_EPS) * (gain * math.sqrt(n))


def _sigmoid(x):
    return 0.5 * jnp.tanh(0.5 * x) + 0.5


def _silu(x):
    hx = 0.5 * x
    return hx * jnp.tanh(hx) + hx


def _block_diag(a, b):
    zero = jnp.zeros_like(a)
    return jnp.concatenate([jnp.concatenate([a, zero], axis=1), jnp.concatenate([zero, b], axis=1)], axis=0)


def _spread_heads(x):
    lane = lax.broadcasted_iota(jnp.int32, (x.shape[0], D_HEAD), 1)
    per_tile = D_HEAD // DK_B
    tiles = []
    for h in range(N_HEADS):
        t = x[:, (h // per_tile) * D_HEAD:(h // per_tile + 1) * D_HEAD]
        shift = (h % per_tile) * DK_B
        if shift:
            t = pltpu.roll(t, D_HEAD - shift, axis=1)
        tiles.append(jnp.where(lane < DK_B, t, 0.0))
    return jnp.concatenate(tiles, axis=1)


def _level_constants():
    t = np.arange(CH)
    mats = []
    for l in range(N_MXU_LEVELS):
        s = 1 << l
        ref = (t & ~(2 * s - 1)) + s - 1
        odd = (t & s) != 0
        u = t[None, :]
        a = np.where(odd[:, None], (u > ref[:, None]) & (u <= t[:, None]),
                     (u > t[:, None]) & (u <= ref[:, None]))
        mats.append(a)
    mats.append(t[None, :] <= t[:, None])
    a_all = np.concatenate(mats, axis=0).astype(np.float32)
    a_cat = np.concatenate([a_all, a_all], axis=1)
    x = t[:, None] ^ t[None, :]
    lvl = np.where(x > 0, np.floor(np.log2(np.maximum(x, 1))).astype(np.int32), LVL_DIAG)
    lvl = np.where(t[:, None] >= t[None, :], lvl, LVL_NONE).astype(np.int32)
    return a_cat, np.concatenate([lvl, lvl], axis=1)


LVL_NONE = -1
LVL_DIAG = -2
MERGED_ROWS = BF16_ROWS
MERGED_LEVELS = MERGED_ROWS.bit_length() - 1
MERGED_MAX_EXP = 96.0
MERGED_MAX_KEY = 2.0 ** 24
EPILOGUE_PIECE = 512
GATE_PIECE = 512
N_PREP_STAGES = 1 + N_HEADS
N_PAIR_STAGES = N_HEADS // 2


def _gla_chunk(inputs_fn, v_fn, gate_fn, gain, amat_ref, lvl2, e_ref, qk_ref, w_ref, sc_ref, st_ref, o_ref, o_off,
               rows, merged):
    q, k, v, g2 = inputs_fn()
    qk_ref[0] = q
    qk_ref[1] = k
    w_ref[ROW_V] = v.astype(_BF16)
    g_hi = g2.astype(_BF16)
    g_lo = (g2 - g_hi.astype(_F32)).astype(_BF16)
    cum = slice(N_MXU_LEVELS * CH, (N_MXU_LEVELS + 1) * CH)
    need = cum if merged >= N_MXU_LEVELS else slice(0, cum.stop)
    e_ref[need] = _dot(amat_ref[need, :], jnp.concatenate([g_hi, g_lo], axis=0))
    yield

    sub = lax.broadcasted_iota(jnp.int32, (CH, D_HEAD), 0)
    odd_rows = [(sub & (1 << l)) != 0 for l in range(N_MXU_LEVELS)]
    d_last = []
    for ct in range(N_HEADS):
        cs = slice(ct * D_HEAD, (ct + 1) * D_HEAD)
        qc = qk_ref[0, :, cs]
        kc = qk_ref[1, :, cs]
        big_g = e_ref[cum, cs]
        g_last = big_g[CH - 1:CH, :]
        if merged:
            blk = 1 << merged
            n_blk = CH // blk
            g_prev = [big_g[b * blk - 1:b * blk, :] if b else jnp.zeros((1, D_HEAD), _F32) for b in range(n_blk)]
            q_blk, k_blk = [], []
            for b in range(n_blk):
                rel = big_g[b * blk:(b + 1) * blk] - g_prev[b]
                q_blk.append(qc[b * blk:(b + 1) * blk] * jnp.exp2(rel))
                k_blk.append(kc[b * blk:(b + 1) * blk] * jnp.exp2(-rel))

            def put(row, parts):
                w_ref[row, :, cs] = jnp.concatenate(parts, axis=0).astype(_BF16)

            put(ROW_SUB_Q, q_blk)
            put(ROW_SUB_K, k_blk)
            for l in range(merged, N_LEVELS):
                per = (1 << l) // blk
                parts = []
                for b in range(n_blk):
                    first = (b // per) * per
                    if (b // per) % 2:
                        parts.append(q_blk[b] * jnp.exp2(g_prev[b] - g_prev[first]))
                    else:
                        parts.append(k_blk[b] * jnp.exp2(g_prev[first + per] - g_prev[b]))
                put(l, parts)
            put(ROW_QT, [q_blk[b] * jnp.exp2(g_prev[b]) for b in range(n_blk)])
            put(ROW_KH, [k_blk[b] * jnp.exp2(g_last - g_prev[b]) for b in range(n_blk)])
            d_last.append(jnp.exp2(g_last))
            yield
            continue
        for l in range(N_LEVELS):
            s = 1 << l
            if l < N_MXU_LEVELS:
                w = jnp.where(odd_rows[l], qc, kc) * jnp.exp2(e_ref[l * CH:(l + 1) * CH, cs])
            else:
                parts = []
                for base in range(0, CH, 2 * s):
                    mid = base + s
                    g_ref_row = jnp.broadcast_to(big_g[mid - 1:mid, :], (s, D_HEAD))
                    parts.append(kc[base:mid] * jnp.exp2(g_ref_row - big_g[base:mid]))
                    parts.append(qc[mid:mid + s] * jnp.exp2(big_g[mid:mid + s] - g_ref_row))
                w = jnp.concatenate(parts, axis=0)
            w_ref[l, :, cs] = w.astype(_BF16)
        w_ref[ROW_QT, :, cs] = (qc * jnp.exp2(big_g)).astype(_BF16)
        w_ref[ROW_KH, :, cs] = (kc * jnp.exp2(g_last - big_g)).astype(_BF16)
        d_last.append(jnp.exp2(g_last))
        yield

    for pr in range(N_HEADS // 2):
        h0 = slice(pr * D_PAIR, pr * D_PAIR + D_HEAD)
        h1 = slice(pr * D_PAIR + D_HEAD, (pr + 1) * D_PAIR)
        ps = slice(pr * D_PAIR, (pr + 1) * D_PAIR)

        def keys(idx, lanes, s):
            if s < BF16_ROWS:
                return w_ref[idx, :, lanes]
            zero = jnp.zeros((s, D_HEAD), _BF16)
            return jnp.concatenate([zero if b % 2 else w_ref[idx, b * s:(b + 1) * s, lanes]
                                    for b in range(CH // s)], axis=0)

        low = jnp.zeros((CH, D_PAIR), _F32)
        if merged:
            r = _dot_nt(w_ref[ROW_SUB_Q, :, ps], _block_diag(w_ref[ROW_SUB_K, :, h0], w_ref[ROW_SUB_K, :, h1]))
            sc_ref[pr] = jnp.where((lvl2 != LVL_NONE) & (lvl2 < merged), r, low)
        for l in range(merged, N_LEVELS):
            s = 1 << l
            rhs = _block_diag(keys(l, h0, s), keys(l, h1, s))
            odd_blocks = [b for b in range(CH // s) if b % 2]
            if s >= BF16_ROWS:
                lhs = jnp.concatenate([w_ref[l, b * s:(b + 1) * s, ps] for b in odd_blocks], axis=0)
            else:
                lhs = w_ref[l, :, ps]
            r = _dot_nt(lhs, rhs)
            if s < SUBLANES:
                low = jnp.where(lvl2 == l, r, low)
                if 2 * s == SUBLANES:
                    sc_ref[pr] = low
                continue
            for i, b in enumerate(odd_blocks):
                src_row = i * s if s >= BF16_ROWS else b * s
                for head in range(2):
                    c0 = head * D_HEAD + (b - 1) * s
                    sc_ref[pr, b * s:(b + 1) * s, c0:c0 + s] = r[src_row:src_row + s, c0:c0 + s]
        sc = sc_ref[pr].astype(_BF16)

        st0 = st_ref[2 * pr]
        st1 = st_ref[2 * pr + 1]
        o = (_dot(sc, _block_diag(w_ref[ROW_V, :, h0], w_ref[ROW_V, :, h1]))
             + _dot_nt(w_ref[ROW_QT, :, ps], _block_diag(st0.astype(_BF16), st1.astype(_BF16))))
        st_ref[2 * pr] = st0 * d_last[2 * pr] + _dot_tn(w_ref[ROW_V, :, h0], w_ref[ROW_KH, :, h0])
        st_ref[2 * pr + 1] = st1 * d_last[2 * pr + 1] + _dot_tn(w_ref[ROW_V, :, h1], w_ref[ROW_KH, :, h1])
        for i, hs in enumerate((h0, h1)):
            o_h = o[:, i * D_HEAD:(i + 1) * D_HEAD]
            if not merged:
                self_score = jnp.sum(qk_ref[0, :, hs] * qk_ref[1, :, hs], axis=-1, keepdims=True)
                o_h = o_h + self_score * v_fn(hs)
            o_ref[rows, o_off + hs.start:o_off + hs.stop] = _rms(o_h, gain[:, hs]) * _silu(gate_fn(hs))
        yield


STAGE_ROWS = TL // 2
N_LOAD_STAGES = 8


def _load_weights(jobs, zero_fills, stages, sem):
    depth = len(stages)

    def copy(i):
        src, row, n, _, _ = jobs[i]
        return pltpu.make_async_copy(src.at[pl.ds(row, n)], stages[i % depth](n), sem.at[i % depth])

    for i in range(min(depth - 1, len(jobs))):
        copy(i).start()
    for i, (_, _, n, dst, dst_row) in enumerate(jobs):
        if i + depth - 1 < len(jobs):
            copy(i + depth - 1).start()
        copy(i).wait()
        dst[dst_row:dst_row + n, :] = stages[i % depth](n)[...].astype(_BF16)
    for dst, row, n in zero_fills:
        dst[row:row + n, :] = jnp.zeros((n, dst.shape[1]), _BF16)


def _weight_jobs(w_in_hbm, w_in_ref, others):
    src_off = np.cumsum((0,) + IN_WIDTHS)
    qa, fa, ia, ga, qb, kb, vb, gb, ab, za, zb = (int(o) for o in src_off[:-1])
    groups = [(OFF_QA, qa, D_BR), (OFF_FA, fa, D_BR), (OFF_IA, ia, D_BR), (OFF_GA, ga, D_BR),
              (OFF_QB, qb, D_QK_B), (OFF_KB, kb, D_QK_B), (OFF_VB, vb, D_BR), (OFF_GB, gb, D_BR),
              (OFF_AB, ab, GLA_RANK), (OFF_ZA, za, D_MODEL), (OFF_ZB, zb, D_MODEL)]
    jobs = []
    for dst_row, src_row, n in groups:
        for r in range(0, n, STAGE_ROWS):
            jobs.append((w_in_hbm, src_row + r, min(STAGE_ROWS, n - r), w_in_ref, dst_row + r))
    for hbm, ref in others:
        for r in range(0, ref.shape[0], STAGE_ROWS):
            jobs.append((hbm, r, min(STAGE_ROWS, ref.shape[0] - r), ref, r))
    return jobs, [(w_in_ref, OFF_AB + GLA_RANK, RANK_PAD - GLA_RANK)]


def _block_kernel(x_ref, p_ref, w_in_hbm, lb_ref, w_up_in_ref, b_gla_in_ref, n_pre_ref, n_post_ref,
                  hn_a_ref, hn_b_ref, w_ba_hbm, w_bb_hbm, w_out_hbm, w_ple_hbm, w_pg_hbm,
                  n_ple_ref, amat_ref, lvl_ref, out_ref,
                  proj_ref, gz_ref, e_ref, qka_ref, qkb_ref, wa_ref, wb_ref, sca_ref, scb_ref, st_a_ref, st_b_ref,
                  w_in_ref, w_ba_ref, w_bb_ref, w_out_ref, w_ple_ref, w_pg_ref, w_up_ref, b_gla_ref, load_sem):
    @pl.when(jnp.logical_and(pl.program_id(0) == 0, pl.program_id(1) == 0))
    def _():
        jobs, zero_fills = _weight_jobs(w_in_hbm, w_in_ref, ((w_ba_hbm, w_ba_ref), (w_bb_hbm, w_bb_ref),
                                                           (w_out_hbm, w_out_ref), (w_ple_hbm, w_ple_ref),
                                                           (w_pg_hbm, w_pg_ref)))
        stages = [lambda n, r=r: out_ref.at[0, pl.ds(r, n), :] for r in range(0, TL, STAGE_ROWS)]
        stages += [lambda n, r=r, c=c: proj_ref.at[pl.ds(r, n), pl.ds(c, D_MODEL)]
                   for c in range(0, D_PROJ - D_MODEL + 1, D_MODEL) for r in range(0, TL, STAGE_ROWS)]
        _load_weights(jobs, zero_fills, stages[:N_LOAD_STAGES], load_sem)
        w_up_ref[...] = jnp.zeros(w_up_ref.shape, _BF16)
        w_up_ref[0:GLA_RANK, :] = _spread_heads(w_up_in_ref[0]).astype(_BF16)
        bias = _spread_heads(jnp.broadcast_to(b_gla_in_ref[...], (SUBLANES, D_QK_B)))
        b_gla_ref[...] = bias[0:1, :]

    @pl.when(pl.program_id(1) == 0)
    def _():
        st_a_ref[...] = jnp.zeros_like(st_a_ref)
        st_b_ref[...] = jnp.zeros_like(st_b_ref)

    x = x_ref[0]
    h = _rms(x, n_pre_ref[...]).astype(_BF16)
    proj_ref[:, OFF_QB:D_PROJ] = _dot_nt(h, w_in_ref[OFF_QB:D_PROJ, :])

    def gate_pieces():
        def piece(a):
            def run():
                z = _dot_nt(h, w_in_ref[D_PROJ + a:D_PROJ + a + GATE_PIECE, :])
                gz_ref[:, a:a + GATE_PIECE] = _sigmoid(z).astype(_BF16)
            return run
        return [piece(a) for a in range(0, 2 * D_MODEL, GATE_PIECE)]

    logits = lb_ref[...]
    ex = jnp.exp(logits - jnp.max(logits, axis=0, keepdims=True))
    lb = ex[0:1, :] / jnp.sum(ex, axis=0, keepdims=True)
    fg_half = 0.5 * (1.0 - lb)
    fg_mid = lb + fg_half
    lvl2 = lvl_ref[...]

    step_exp = MERGED_MAX_EXP / MERGED_ROWS
    z_limit = step_exp * GLA_TAU / LOG2_E - math.log(2.0)
    a_max = jnp.max(jnp.abs(proj_ref[:, OFF_AB:OFF_AB + RANK_PAD].astype(_BF16).astype(_F32)))
    up_max = jnp.max(jnp.sum(jnp.abs(w_up_ref[...].astype(_F32)), axis=0, keepdims=True))
    z_bound = a_max * up_max + jnp.max(jnp.abs(b_gla_ref[...]))
    safe = jnp.logical_and(
        jnp.logical_and(jnp.min(lb) > 2.0 ** -step_exp, z_bound < z_limit),
        jnp.max(jnp.abs(proj_ref[:, OFF_KB:OFF_KB + D_QK_B])) < MERGED_MAX_KEY)

    def chunk_phase(merged):
        streams = []
        for c in range(TL // CH):
            rows = slice(c * CH, (c + 1) * CH)

            def col(off, width=D_BR, rows=rows):
                return proj_ref[rows, off:off + width]

            def inputs_a(col=col):
                forget = fg_mid + fg_half * jnp.tanh(0.5 * col(OFF_FA))
                return _silu(col(OFF_QA)) * (DK_A ** -0.5), 1.0 - forget, col(OFF_IA), jnp.log2(forget)

            def inputs_b(col=col):
                z = _dot(col(OFF_AB, RANK_PAD).astype(_BF16), w_up_ref[...]) + b_gla_ref[...]
                log2_alpha = (jnp.maximum(-z, 0.0) + jnp.log(1.0 + jnp.exp(-jnp.abs(z)))) * (-LOG2_E / GLA_TAU)
                q = _spread_heads(col(OFF_QB, D_QK_B)) * (DK_B ** -0.5)
                return q, _spread_heads(col(OFF_KB, D_QK_B)), col(OFF_VB), log2_alpha

            def lanes_of(off, col=col):
                return lambda hs: col(off + hs.start, hs.stop - hs.start)

            streams.append(_gla_chunk(inputs_a, lanes_of(OFF_IA), lanes_of(OFF_GA), hn_a_ref[...], amat_ref, lvl2,
                                      e_ref, qka_ref, wa_ref, sca_ref, st_a_ref, proj_ref, OFF_QA, rows, merged))
            streams.append(_gla_chunk(inputs_b, lanes_of(OFF_VB), lanes_of(OFF_GB), hn_b_ref[...], amat_ref, lvl2,
                                      e_ref, qkb_ref, wb_ref, scb_ref, st_b_ref, proj_ref, OFF_QB, rows, merged))

        def advance(stream, n):
            for _ in range(n):
                next(stream, None)

        share = [N_PREP_STAGES // N_PAIR_STAGES + (j < N_PREP_STAGES % N_PAIR_STAGES)
                 for j in range(N_PAIR_STAGES)]
        gates = gate_pieces()
        for j in range(max(N_PREP_STAGES, len(gates))):
            if j < len(gates):
                gates[j]()
            if j < N_PREP_STAGES:
                advance(streams[0], 1)
        for i, cur in enumerate(streams):
            nxt = streams[i + 1] if i + 1 < len(streams) else None
            for j in range(N_PAIR_STAGES):
                if nxt is not None:
                    advance(nxt, 1)
                advance(cur, 1)
                if nxt is not None:
                    advance(nxt, share[j] - 1)

    proj_ref[:, 0:OFF_QB] = _dot_nt(h, w_in_ref[0:OFF_QB, :])

    pl.when(safe)(lambda: chunk_phase(MERGED_LEVELS))
    pl.when(jnp.logical_not(safe))(lambda: chunk_phase(0))

    ya = _dot(proj_ref[:, OFF_QA:OFF_QA + D_BR].astype(_BF16), w_ba_ref[...])
    yb = _dot(proj_ref[:, OFF_QB:OFF_QB + D_BR].astype(_BF16), w_bb_ref[...])
    y = gz_ref[:, 0:D_MODEL] * ya + gz_ref[:, D_MODEL:2 * D_MODEL] * yb
    y = _dot(y.astype(_BF16), w_out_ref[...])
    x1 = x + _rms(y, n_post_ref[...])
    e = _rms(_dot(p_ref[0, 0].astype(_BF16), w_ple_ref[...]), n_ple_ref[...])
    x1b = x1.astype(_BF16)
    for c in range(0, D_MODEL, EPILOGUE_PIECE):
        cols = slice(c, c + EPILOGUE_PIECE)
        out_ref[0, :, cols] = x1[:, cols] + _sigmoid(_dot(x1b, w_pg_ref[:, cols])) * e[:, cols]


def _const_spec(shape):
    nd = len(shape)
    return pl.BlockSpec(shape, lambda b, t: (0,) * nd, pipeline_mode=pl.Buffered(1))


def kernel(x, p, w_in, lb_logits, w_gla_up, b_gla, norm_pre, norm_post, head_norm_a, head_norm_b,
           w_branch_a, w_branch_b, w_out, w_ple, w_ple_gate, norm_ple):
    bsz, seq, _ = x.shape
    assert x.shape == (bsz, seq, D_MODEL) and seq % TL == 0 and w_in.shape[0] == 1 and p.shape[0] == 1

    assert w_in.shape == (1, D_MODEL, sum(IN_WIDTHS))
    weights = (w_in[0].T, w_branch_a[0], w_branch_b[0], w_out[0], w_ple[0], w_ple_gate[0])
    for w in weights:
        assert w.shape[1] == D_MODEL and w.shape[0] % BF16_ROWS == 0

    a_cat, lvl2 = _level_constants()
    amat = jnp.asarray(a_cat, dtype=_BF16)
    lvl2 = jnp.asarray(lvl2)

    row = lambda a: a[0][None, :]
    operands = (
        x, p, weights[0], lb_logits, w_gla_up, b_gla, row(norm_pre), row(norm_post), row(head_norm_a), row(head_norm_b),
        *weights[1:], row(norm_ple), amat, lvl2,
    )
    in_hbm = {2, 10, 11, 12, 13, 14}
    in_specs = [
        pl.BlockSpec((1, TL, D_MODEL), lambda b, t: (b, t, 0)),
        pl.BlockSpec((1, 1, TL, D_PLE), lambda b, t: (0, b, t, 0)),
    ] + [pl.BlockSpec(memory_space=pl.ANY) if i in in_hbm else _const_spec(a.shape)
         for i, a in enumerate(operands) if i >= 2]

    return pl.pallas_call(
        _block_kernel,
        grid=(bsz, seq // TL),
        in_specs=in_specs,
        out_specs=pl.BlockSpec((1, TL, D_MODEL), lambda b, t: (b, t, 0)),
        out_shape=jax.ShapeDtypeStruct(x.shape, x.dtype),
        scratch_shapes=[
            pltpu.VMEM((TL, D_PROJ), _F32),
            pltpu.VMEM((TL, 2 * D_MODEL), _BF16),
            pltpu.VMEM(((N_MXU_LEVELS + 1) * CH, D_BR), _F32),
            pltpu.VMEM((2, CH, D_BR), _F32),
            pltpu.VMEM((2, CH, D_BR), _F32),
            pltpu.VMEM((N_OPERAND_ROWS, CH, D_BR), _BF16),
            pltpu.VMEM((N_OPERAND_ROWS, CH, D_BR), _BF16),
            pltpu.VMEM((N_HEADS // 2, CH, D_PAIR), _F32),
            pltpu.VMEM((N_HEADS // 2, CH, D_PAIR), _F32),
            pltpu.VMEM((N_HEADS, D_HEAD, D_HEAD), _F32),
            pltpu.VMEM((N_HEADS, D_HEAD, D_HEAD), _F32),
            pltpu.VMEM((D_IN_PAD, D_MODEL), _BF16),
            pltpu.VMEM((D_BR, D_MODEL), _BF16),
            pltpu.VMEM((D_BR, D_MODEL), _BF16),
            pltpu.VMEM((D_MODEL, D_MODEL), _BF16),
            pltpu.VMEM((D_PLE, D_MODEL), _BF16),
            pltpu.VMEM((D_MODEL, D_MODEL), _BF16),
            pltpu.VMEM((RANK_PAD, D_BR), _BF16),
            pltpu.VMEM((1, D_BR), _F32),
            pltpu.SemaphoreType.DMA((N_LOAD_STAGES,)),
        ],
        compiler_params=pltpu.CompilerParams(
            dimension_semantics=("arbitrary", "arbitrary"),
            vmem_limit_bytes=V7X_VMEM_LIMIT_BYTES,
        ),
        name="hgrn2_gla_block",
    )(*operands)
```

```python
import math

import numpy as np
import jax
import jax.numpy as jnp
from jax import lax
from jax.experimental import pallas as pl
from jax.experimental.pallas import tpu as pltpu

D_MODEL = 1024
D_PLE = 256
N_HEADS = 4
D_HEAD = 128
D_BR = N_HEADS * D_HEAD
D_PAIR = 2 * D_HEAD
DK_A = 128
DK_B = 64
GLA_RANK = 16
GLA_TAU = 16.0
NORM_EPS = 1e-6
LOG2_E = 1.4426950408889634

SUBLANES = 8
BF16_ROWS = 16
N_MXU_LEVELS = SUBLANES.bit_length() - 1

CH = 128
N_LEVELS = CH.bit_length() - 1
TL = 512
RANK_PAD = 128
IN_WIDTHS = (D_BR, D_BR, D_BR, D_BR, N_HEADS * DK_B, N_HEADS * DK_B, D_BR, D_BR, GLA_RANK, D_MODEL, D_MODEL)

ROW_SUB_Q = 0
ROW_SUB_K = 1
ROW_QT = N_LEVELS
ROW_KH = N_LEVELS + 1
ROW_V = N_LEVELS + 2
N_OPERAND_ROWS = N_LEVELS + 3

OFF_QA = 0
OFF_FA = OFF_QA + D_BR
OFF_IA = OFF_FA + D_BR
OFF_GA = OFF_IA + D_BR
D_QK_B = N_HEADS * DK_B
OFF_QB = OFF_GA + D_BR
OFF_KB = OFF_QB + D_QK_B
OFF_VB = OFF_KB + D_QK_B
OFF_GB = OFF_VB + D_BR
OFF_AB = OFF_GB + D_BR
D_PROJ = OFF_AB + RANK_PAD
OFF_ZA = D_PROJ
OFF_ZB = OFF_ZA + D_MODEL
D_IN_PAD = OFF_ZB + D_MODEL

V7X_VMEM_LIMIT_BYTES = 57 * 1024 * 1024

_F32 = jnp.float32
_BF16 = jnp.bfloat16


def _dot(a, b):
    return jnp.dot(a, b, preferred_element_type=_F32)


def _dot_nt(a, b):
    return lax.dot_general(a, b, (((1,), (1,)), ((), ())), preferred_element_type=_F32)


def _dot_tn(a, b):
    return lax.dot_general(a, b, (((0,), (0,)), ((), ())), preferred_element_type=_F32)


def _rms(x, gain):
    n = x.shape[-1]
    ss = jnp.sum(x * x, axis=-1, keepdims=True)
    return x * lax.rsqrt(ss + n * NORM_EPS) * (gain * math.sqrt(n))


def _sigmoid(x):
    return 0.5 * jnp.tanh(0.5 * x) + 0.5


def _silu(x):
    hx = 0.5 * x
    return hx * jnp.tanh(hx) + hx


def _block_diag(a, b):
    zero = jnp.zeros_like(a)
    return jnp.concatenate([jnp.concatenate([a, zero], axis=1), jnp.concatenate([zero, b], axis=1)], axis=0)


def _spread_heads(x):
    lane = lax.broadcasted_iota(jnp.int32, (x.shape[0], D_HEAD), 1)
    per_tile = D_HEAD // DK_B
    tiles = []
    for h in range(N_HEADS):
        t = x[:, (h // per_tile) * D_HEAD:(h // per_tile + 1) * D_HEAD]
        shift = (h % per_tile) * DK_B
        if shift:
            t = pltpu.roll(t, D_HEAD - shift, axis=1)
        tiles.append(jnp.where(lane < DK_B, t, 0.0))
    return jnp.concatenate(tiles, axis=1)


def _level_constants():
    t = np.arange(CH)
    mats = []
    for l in range(N_MXU_LEVELS):
        s = 1 << l
        ref = (t & ~(2 * s - 1)) + s - 1
        odd = (t & s) != 0
        u = t[None, :]
        a = np.where(odd[:, None], (u > ref[:, None]) & (u <= t[:, None]),
                     (u > t[:, None]) & (u <= ref[:, None]))
        mats.append(a)
    mats.append(t[None, :] <= t[:, None])
    a_all = np.concatenate(mats, axis=0).astype(np.float32)
    a_cat = np.concatenate([a_all, a_all], axis=1)
    x = t[:, None] ^ t[None, :]
    lvl = np.where(x > 0, np.floor(np.log2(np.maximum(x, 1))).astype(np.int32), LVL_DIAG)
    lvl = np.where(t[:, None] >= t[None, :], lvl, LVL_NONE).astype(np.int32)
    return a_cat, np.concatenate([lvl, lvl], axis=1)


LVL_NONE = -1
LVL_DIAG = -2
MERGED_ROWS = BF16_ROWS
MERGED_LEVELS = MERGED_ROWS.bit_length() - 1
MERGED_MAX_EXP = 96.0
MERGED_MAX_KEY = 2.0 ** 24
EPILOGUE_PIECE = 512
GATE_PIECE = 512
N_PREP_STAGES = 1 + N_HEADS
N_PAIR_STAGES = N_HEADS // 2


def _gla_chunk(inputs_fn, v_fn, gate_fn, gain, amat_ref, lvl2, e_ref, qk_ref, w_ref, sc_ref, st_ref, o_ref, o_off,
               rows, merged):
    q, k, v, g2 = inputs_fn()
    qk_ref[0] = q
    qk_ref[1] = k
    w_ref[ROW_V] = v.astype(_BF16)
    g_hi = g2.astype(_BF16)
    g_lo = (g2 - g_hi.astype(_F32)).astype(_BF16)
    cum = slice(N_MXU_LEVELS * CH, (N_MXU_LEVELS + 1) * CH)
    need = cum if merged >= N_MXU_LEVELS else slice(0, cum.stop)
    e_ref[need] = _dot(amat_ref[need, :], jnp.concatenate([g_hi, g_lo], axis=0))
    yield

    sub = lax.broadcasted_iota(jnp.int32, (CH, D_HEAD), 0)
    odd_rows = [(sub & (1 << l)) != 0 for l in range(N_MXU_LEVELS)]
    d_last = []
    for ct in range(N_HEADS):
        cs = slice(ct * D_HEAD, (ct + 1) * D_HEAD)
        qc = qk_ref[0, :, cs]
        kc = qk_ref[1, :, cs]
        big_g = e_ref[cum, cs]
        g_last = big_g[CH - 1:CH, :]
        if merged:
            blk = 1 << merged
            n_blk = CH // blk
            g_prev = [big_g[b * blk - 1:b * blk, :] if b else jnp.zeros((1, D_HEAD), _F32) for b in range(n_blk)]
            q_blk, k_blk = [], []
            for b in range(n_blk):
                rel = big_g[b * blk:(b + 1) * blk] - g_prev[b]
                q_blk.append(qc[b * blk:(b + 1) * blk] * jnp.exp2(rel))
                k_blk.append(kc[b * blk:(b + 1) * blk] * jnp.exp2(-rel))

            def put(row, parts):
                w_ref[row, :, cs] = jnp.concatenate(parts, axis=0).astype(_BF16)

            put(ROW_SUB_Q, q_blk)
            put(ROW_SUB_K, k_blk)
            for l in range(merged, N_LEVELS):
                per = (1 << l) // blk
                parts = []
                for b in range(n_blk):
                    first = (b // per) * per
                    if (b // per) % 2:
                        parts.append(q_blk[b] * jnp.exp2(g_prev[b] - g_prev[first]))
                    else:
                        parts.append(k_blk[b] * jnp.exp2(g_prev[first + per] - g_prev[b]))
                put(l, parts)
            put(ROW_QT, [q_blk[b] * jnp.exp2(g_prev[b]) for b in range(n_blk)])
            put(ROW_KH, [k_blk[b] * jnp.exp2(g_last - g_prev[b]) for b in range(n_blk)])
            d_last.append(jnp.exp2(g_last))
            yield
            continue
        for l in range(N_LEVELS):
            s = 1 << l
            if l < N_MXU_LEVELS:
                w = jnp.where(odd_rows[l], qc, kc) * jnp.exp2(e_ref[l * CH:(l + 1) * CH, cs])
            else:
                parts = []
                for base in range(0, CH, 2 * s):
                    mid = base + s
                    g_ref_row = jnp.broadcast_to(big_g[mid - 1:mid, :], (s, D_HEAD))
                    parts.append(kc[base:mid] * jnp.exp2(g_ref_row - big_g[base:mid]))
                    parts.append(qc[mid:mid + s] * jnp.exp2(big_g[mid:mid + s] - g_ref_row))
                w = jnp.concatenate(parts, axis=0)
            w_ref[l, :, cs] = w.astype(_BF16)
        w_ref[ROW_QT, :, cs] = (qc * jnp.exp2(big_g)).astype(_BF16)
        w_ref[ROW_KH, :, cs] = (kc * jnp.exp2(g_last - big_g)).astype(_BF16)
        d_last.append(jnp.exp2(g_last))
        yield

    for pr in range(N_HEADS // 2):
        h0 = slice(pr * D_PAIR, pr * D_PAIR + D_HEAD)
        h1 = slice(pr * D_PAIR + D_HEAD, (pr + 1) * D_PAIR)
        ps = slice(pr * D_PAIR, (pr + 1) * D_PAIR)

        def keys(idx, lanes, s):
            if s < BF16_ROWS:
                return w_ref[idx, :, lanes]
            zero = jnp.zeros((s, D_HEAD), _BF16)
            return jnp.concatenate([zero if b % 2 else w_ref[idx, b * s:(b + 1) * s, lanes]
                                    for b in range(CH // s)], axis=0)

        low = jnp.zeros((CH, D_PAIR), _F32)
        if merged:
            r = _dot_nt(w_ref[ROW_SUB_Q, :, ps], _block_diag(w_ref[ROW_SUB_K, :, h0], w_ref[ROW_SUB_K, :, h1]))
            sc_ref[pr] = jnp.where((lvl2 != LVL_NONE) & (lvl2 < merged), r, low)
        for l in range(merged, N_LEVELS):
            s = 1 << l
            rhs = _block_diag(keys(l, h0, s), keys(l, h1, s))
            odd_blocks = [b for b in range(CH // s) if b % 2]
            if s >= BF16_ROWS:
                lhs = jnp.concatenate([w_ref[l, b * s:(b + 1) * s, ps] for b in odd_blocks], axis=0)
            else:
                lhs = w_ref[l, :, ps]
            r = _dot_nt(lhs, rhs)
            if s < SUBLANES:
                low = jnp.where(lvl2 == l, r, low)
                if 2 * s == SUBLANES:
                    sc_ref[pr] = low
                continue
            for i, b in enumerate(odd_blocks):
                src_row = i * s if s >= BF16_ROWS else b * s
                for head in range(2):
                    c0 = head * D_HEAD + (b - 1) * s
                    sc_ref[pr, b * s:(b + 1) * s, c0:c0 + s] = r[src_row:src_row + s, c0:c0 + s]
        sc = sc_ref[pr].astype(_BF16)

        st0 = st_ref[2 * pr]
        st1 = st_ref[2 * pr + 1]
        o = (_dot(sc, _block_diag(w_ref[ROW_V, :, h0], w_ref[ROW_V, :, h1]))
             + _dot_nt(w_ref[ROW_QT, :, ps], _block_diag(st0.astype(_BF16), st1.astype(_BF16))))
        st_ref[2 * pr] = st0 * d_last[2 * pr] + _dot_tn(w_ref[ROW_V, :, h0], w_ref[ROW_KH, :, h0])
        st_ref[2 * pr + 1] = st1 * d_last[2 * pr + 1] + _dot_tn(w_ref[ROW_V, :, h1], w_ref[ROW_KH, :, h1])
        for i, hs in enumerate((h0, h1)):
            o_h = o[:, i * D_HEAD:(i + 1) * D_HEAD]
            if not merged:
                self_score = jnp.sum(qk_ref[0, :, hs] * qk_ref[1, :, hs], axis=-1, keepdims=True)
                o_h = o_h + self_score * v_fn(hs)
            o_ref[rows, o_off + hs.start:o_off + hs.stop] = _rms(o_h, gain[:, hs]) * _silu(gate_fn(hs))
        yield


STAGE_ROWS = TL // 2
N_LOAD_STAGES = 8
N_DMA_PRIORITIES = 2


def _load_weights(jobs, zero_fills, stages, sem):
    depth = len(stages)

    def copy(i):
        src, row, n, _, _ = jobs[i]
        return pltpu.make_async_copy(src.at[pl.ds(row, n)], stages[i % depth](n), sem.at[i % depth])

    def start(i):
        copy(i).start(priority=i % N_DMA_PRIORITIES)

    for i in range(min(depth - 1, len(jobs))):
        start(i)
    for i, (_, _, n, dst, dst_row) in enumerate(jobs):
        if i + depth - 1 < len(jobs):
            start(i + depth - 1)
        copy(i).wait()
        dst[dst_row:dst_row + n, :] = stages[i % depth](n)[...].astype(_BF16)
    for dst, row, n in zero_fills:
        dst[row:row + n, :] = jnp.zeros((n, dst.shape[1]), _BF16)


def _weight_jobs(w_in_hbm, w_in_ref, others):
    src_off = np.cumsum((0,) + IN_WIDTHS)
    qa, fa, ia, ga, qb, kb, vb, gb, ab, za, zb = (int(o) for o in src_off[:-1])
    groups = [(OFF_QA, qa, D_BR), (OFF_FA, fa, D_BR), (OFF_IA, ia, D_BR), (OFF_GA, ga, D_BR),
              (OFF_QB, qb, D_QK_B), (OFF_KB, kb, D_QK_B), (OFF_VB, vb, D_BR), (OFF_GB, gb, D_BR),
              (OFF_AB, ab, GLA_RANK), (OFF_ZA, za, D_MODEL), (OFF_ZB, zb, D_MODEL)]
    jobs = []
    for dst_row, src_row, n in groups:
        for r in range(0, n, STAGE_ROWS):
            jobs.append((w_in_hbm, src_row + r, min(STAGE_ROWS, n - r), w_in_ref, dst_row + r))
    for hbm, ref in others:
        for r in range(0, ref.shape[0], STAGE_ROWS):
            jobs.append((hbm, r, min(STAGE_ROWS, ref.shape[0] - r), ref, r))
    return jobs, [(w_in_ref, OFF_AB + GLA_RANK, RANK_PAD - GLA_RANK)]


def _block_kernel(x_ref, p_ref, w_in_hbm, lb_ref, w_up_in_ref, b_gla_in_ref, n_pre_ref, n_post_ref,
                  hn_a_ref, hn_b_ref, w_ba_hbm, w_bb_hbm, w_out_hbm, w_ple_hbm, w_pg_hbm,
                  n_ple_ref, amat_ref, lvl_ref, out_ref,
                  proj_ref, gz_ref, e_ref, qka_ref, qkb_ref, wa_ref, wb_ref, sca_ref, scb_ref, st_a_ref, st_b_ref,
                  w_in_ref, w_ba_ref, w_bb_ref, w_out_ref, w_ple_ref, w_pg_ref, w_up_ref, b_gla_ref, load_sem):
    @pl.when(jnp.logical_and(pl.program_id(0) == 0, pl.program_id(1) == 0))
    def _():
        jobs, zero_fills = _weight_jobs(w_in_hbm, w_in_ref, ((w_ba_hbm, w_ba_ref), (w_bb_hbm, w_bb_ref),
                                                           (w_out_hbm, w_out_ref), (w_ple_hbm, w_ple_ref),
                                                           (w_pg_hbm, w_pg_ref)))
        stages = [lambda n, r=r: out_ref.at[0, pl.ds(r, n), :] for r in range(0, TL, STAGE_ROWS)]
        stages += [lambda n, r=r, c=c: proj_ref.at[pl.ds(r, n), pl.ds(c, D_MODEL)]
                   for c in range(0, D_PROJ - D_MODEL + 1, D_MODEL) for r in range(0, TL, STAGE_ROWS)]
        _load_weights(jobs, zero_fills, stages[:N_LOAD_STAGES], load_sem)
        w_up_ref[...] = jnp.zeros(w_up_ref.shape, _BF16)
        w_up_ref[0:GLA_RANK, :] = _spread_heads(w_up_in_ref[0]).astype(_BF16)
        bias = _spread_heads(jnp.broadcast_to(b_gla_in_ref[...], (SUBLANES, D_QK_B)))
        b_gla_ref[...] = bias[0:1, :]

    @pl.when(pl.program_id(1) == 0)
    def _():
        st_a_ref[...] = jnp.zeros_like(st_a_ref)
        st_b_ref[...] = jnp.zeros_like(st_b_ref)

    x = x_ref[0]
    h = _rms(x, n_pre_ref[...]).astype(_BF16)
    proj_ref[:, OFF_QB:D_PROJ] = _dot_nt(h, w_in_ref[OFF_QB:D_PROJ, :])

    def gate_pieces():
        def piece(a):
            def run():
                z = _dot_nt(h, w_in_ref[D_PROJ + a:D_PROJ + a + GATE_PIECE, :])
                gz_ref[:, a:a + GATE_PIECE] = _sigmoid(z).astype(_BF16)
            return run
        return [piece(a) for a in range(0, 2 * D_MODEL, GATE_PIECE)]

    logits = lb_ref[...]
    ex = jnp.exp(logits - jnp.max(logits, axis=0, keepdims=True))
    lb = ex[0:1, :] / jnp.sum(ex, axis=0, keepdims=True)
    fg_half = 0.5 * (1.0 - lb)
    fg_mid = lb + fg_half
    lvl2 = lvl_ref[...]

    step_exp = MERGED_MAX_EXP / MERGED_ROWS
    z_limit = step_exp * GLA_TAU / LOG2_E - math.log(2.0)
    a_max = jnp.max(jnp.abs(proj_ref[:, OFF_AB:OFF_AB + RANK_PAD].astype(_BF16).astype(_F32)))
    up_max = jnp.max(jnp.sum(jnp.abs(w_up_ref[...].astype(_F32)), axis=0, keepdims=True))
    z_bound = a_max * up_max + jnp.max(jnp.abs(b_gla_ref[...]))
    safe = jnp.logical_and(
        jnp.logical_and(jnp.min(lb) > 2.0 ** -step_exp, z_bound < z_limit),
        jnp.max(jnp.abs(proj_ref[:, OFF_KB:OFF_KB + D_QK_B])) < MERGED_MAX_KEY)

    def chunk_phase(merged):
        streams = []
        for c in range(TL // CH):
            rows = slice(c * CH, (c + 1) * CH)

            def col(off, width=D_BR, rows=rows):
                return proj_ref[rows, off:off + width]

            def inputs_a(col=col):
                forget = fg_mid + fg_half * jnp.tanh(0.5 * col(OFF_FA))
                return _silu(col(OFF_QA)) * (DK_A ** -0.5), 1.0 - forget, col(OFF_IA), jnp.log2(forget)

            def inputs_b(col=col):
                z = _dot(col(OFF_AB, RANK_PAD).astype(_BF16), w_up_ref[...]) + b_gla_ref[...]
                log2_alpha = (jnp.maximum(-z, 0.0) + jnp.log(1.0 + jnp.exp(-jnp.abs(z)))) * (-LOG2_E / GLA_TAU)
                q = _spread_heads(col(OFF_QB, D_QK_B)) * (DK_B ** -0.5)
                return q, _spread_heads(col(OFF_KB, D_QK_B)), col(OFF_VB), log2_alpha

            def lanes_of(off, col=col):
                return lambda hs: col(off + hs.start, hs.stop - hs.start)

            streams.append(_gla_chunk(inputs_a, lanes_of(OFF_IA), lanes_of(OFF_GA), hn_a_ref[...], amat_ref, lvl2,
                                      e_ref, qka_ref, wa_ref, sca_ref, st_a_ref, proj_ref, OFF_QA, rows, merged))
            streams.append(_gla_chunk(inputs_b, lanes_of(OFF_VB), lanes_of(OFF_GB), hn_b_ref[...], amat_ref, lvl2,
                                      e_ref, qkb_ref, wb_ref, scb_ref, st_b_ref, proj_ref, OFF_QB, rows, merged))

        def advance(stream, n):
            for _ in range(n):
                next(stream, None)

        share = [N_PREP_STAGES // N_PAIR_STAGES + (j < N_PREP_STAGES % N_PAIR_STAGES)
                 for j in range(N_PAIR_STAGES)]
        gates = gate_pieces()
        for j in range(max(N_PREP_STAGES, len(gates))):
            if j < len(gates):
                gates[j]()
            if j < N_PREP_STAGES:
                advance(streams[0], 1)
        for i, cur in enumerate(streams):
            nxt = streams[i + 1] if i + 1 < len(streams) else None
            for j in range(N_PAIR_STAGES):
                if nxt is not None:
                    advance(nxt, 1)
                advance(cur, 1)
                if nxt is not None:
                    advance(nxt, share[j] - 1)

    proj_ref[:, 0:OFF_QB] = _dot_nt(h, w_in_ref[0:OFF_QB, :])

    pl.when(safe)(lambda: chunk_phase(MERGED_LEVELS))
    pl.when(jnp.logical_not(safe))(lambda: chunk_phase(0))

    ya = _dot(proj_ref[:, OFF_QA:OFF_QA + D_BR].astype(_BF16), w_ba_ref[...])
    yb = _dot(proj_ref[:, OFF_QB:OFF_QB + D_BR].astype(_BF16), w_bb_ref[...])
    y = gz_ref[:, 0:D_MODEL] * ya + gz_ref[:, D_MODEL:2 * D_MODEL] * yb
    y = _dot(y.astype(_BF16), w_out_ref[...])
    x1 = x + _rms(y, n_post_ref[...])
    e = _rms(_dot(p_ref[0, 0].astype(_BF16), w_ple_ref[...]), n_ple_ref[...])
    x1b = x1.astype(_BF16)
    for c in range(0, D_MODEL, EPILOGUE_PIECE):
        cols = slice(c, c + EPILOGUE_PIECE)
        out_ref[0, :, cols] = x1[:, cols] + _sigmoid(_dot(x1b, w_pg_ref[:, cols])) * e[:, cols]


def _const_spec(shape):
    nd = len(shape)
    return pl.BlockSpec(shape, lambda b, t: (0,) * nd, pipeline_mode=pl.Buffered(1))


def kernel(x, p, w_in, lb_logits, w_gla_up, b_gla, norm_pre, norm_post, head_norm_a, head_norm_b,
           w_branch_a, w_branch_b, w_out, w_ple, w_ple_gate, norm_ple):
    bsz, seq, _ = x.shape
    assert x.shape == (bsz, seq, D_MODEL) and seq % TL == 0 and w_in.shape[0] == 1 and p.shape[0] == 1

    assert w_in.shape == (1, D_MODEL, sum(IN_WIDTHS))
    weights = (w_in[0].T, w_branch_a[0], w_branch_b[0], w_out[0], w_ple[0], w_ple_gate[0])
    for w in weights:
        assert w.shape[1] == D_MODEL and w.shape[0] % BF16_ROWS == 0

    a_cat, lvl2 = _level_constants()
    amat = jnp.asarray(a_cat, dtype=_BF16)
    lvl2 = jnp.asarray(lvl2)

    row = lambda a: a[0][None, :]
    operands = (
        x, p, weights[0], lb_logits, w_gla_up, b_gla, row(norm_pre), row(norm_post), row(head_norm_a), row(head_norm_b),
        *weights[1:], row(norm_ple), amat, lvl2,
    )
    in_hbm = {2, 10, 11, 12, 13, 14}
    in_specs = [
        pl.BlockSpec((1, TL, D_MODEL), lambda b, t: (b, t, 0)),
        pl.BlockSpec((1, 1, TL, D_PLE), lambda b, t: (0, b, t, 0)),
    ] + [pl.BlockSpec(memory_space=pl.ANY) if i in in_hbm else _const_spec(a.shape)
         for i, a in enumerate(operands) if i >= 2]

    return pl.pallas_call(
        _block_kernel,
        grid=(bsz, seq // TL),
        in_specs=in_specs,
        out_specs=pl.BlockSpec((1, TL, D_MODEL), lambda b, t: (b, t, 0)),
        out_shape=jax.ShapeDtypeStruct(x.shape, x.dtype),
        scratch_shapes=[
            pltpu.VMEM((TL, D_PROJ), _F32),
            pltpu.VMEM((TL, 2 * D_MODEL), _BF16),
            pltpu.VMEM(((N_MXU_LEVELS + 1) * CH, D_BR), _F32),
            pltpu.VMEM((2, CH, D_BR), _F32),
            pltpu.VMEM((2, CH, D_BR), _F32),
            pltpu.VMEM((N_OPERAND_ROWS, CH, D_BR), _BF16),
            pltpu.VMEM((N_OPERAND_ROWS, CH, D_BR), _BF16),
            pltpu.VMEM((N_HEADS // 2, CH, D_PAIR), _F32),
            pltpu.VMEM((N_HEADS // 2, CH, D_PAIR), _F32),
            pltpu.VMEM((N_HEADS, D_HEAD, D_HEAD), _F32),
            pltpu.VMEM((N_HEADS, D_HEAD, D_HEAD), _F32),
            pltpu.VMEM((D_IN_PAD, D_MODEL), _BF16),
            pltpu.VMEM((D_BR, D_MODEL), _BF16),
            pltpu.VMEM((D_BR, D_MODEL), _BF16),
            pltpu.VMEM((D_MODEL, D_MODEL), _BF16),
            pltpu.VMEM((D_PLE, D_MODEL), _BF16),
            pltpu.VMEM((D_MODEL, D_MODEL), _BF16),
            pltpu.VMEM((RANK_PAD, D_BR), _BF16),
            pltpu.VMEM((1, D_BR), _F32),
            pltpu.SemaphoreType.DMA((N_LOAD_STAGES,)),
        ],
        compiler_params=pltpu.CompilerParams(
            dimension_semantics=("arbitrary", "arbitrary"),
            vmem_limit_bytes=V7X_VMEM_LIMIT_BYTES,
        ),
        name="hgrn2_gla_block",
    )(*operands)
```
